```python
import jax
import jax.numpy as jnp
from jax import lax
import numpy as np

D_MODEL = 1024
BATCH = 16
SEQ = 2048
DEPTH = 2

N_BRANCH = 4
HEADS = 4
BRANCH_WIDTH = D_MODEL // 4
HEAD_DV = BRANCH_WIDTH // HEADS
RET_DK = HEAD_DV // 2
GLA_DK = HEAD_DV // 2
HGRN_DK = HEAD_DV
RWKV_N = HEAD_DV
RWKV_W_LORA = 64
RWKV_A_LORA = 64
RWKV_G_LORA = 128
GLA_GATE_LORA = 16
GLA_GATE_TEMP = 16.0
CHUNK = 64
RET_ROPE_BASE = 10000.0
RWKV_DECAY_SCALE = 0.6065306597126334
D_FF = 2816
FFN_CONV_WIDTH = 3
PLE_DIM = 256
LN_EPS = 1e-5
HEAD_EPS = 1e-6
RWKV_GN_EPS = 64e-5
ALPHA = (2.0 * DEPTH) ** 0.25
BETA = (8.0 * DEPTH) ** -0.25

RET_COLS = (HEADS * RET_DK, HEADS * RET_DK, BRANCH_WIDTH, BRANCH_WIDTH)
RWKV_COLS = (BRANCH_WIDTH, BRANCH_WIDTH, BRANCH_WIDTH, RWKV_W_LORA, RWKV_A_LORA, RWKV_G_LORA)
GLA_COLS = (HEADS * GLA_DK, HEADS * GLA_DK, BRANCH_WIDTH, GLA_GATE_LORA, BRANCH_WIDTH)
HGRN_COLS = (HEADS * HGRN_DK, HEADS * HGRN_DK, BRANCH_WIDTH, BRANCH_WIDTH)
GROUP_COLS = (sum(RET_COLS), sum(RWKV_COLS), sum(GLA_COLS), sum(HGRN_COLS), N_BRANCH * D_MODEL)
N_IN = sum(GROUP_COLS)

kernel_name = "hybrid_gated_retnet_rwkv7_gla_hgrn2"


def split_cols(a, sizes):
    return jnp.split(a, [int(c) for c in np.cumsum(sizes)[:-1]], axis=-1)


def heads(a, n_heads=HEADS):
    return a.reshape(*a.shape[:-1], n_heads, a.shape[-1] // n_heads)


def bhsd(a):
    return a.transpose(0, 2, 1, 3)


def shift_right(a):
    return jnp.pad(a, ((0, 0), (1, 0), (0, 0)))[:, :-1]


def layer_norm(x, w, b):
    x32 = x.astype(jnp.float32)
    mu = jnp.mean(x32, -1, keepdims=True)
    var = jnp.mean(jnp.square(x32 - mu), -1, keepdims=True)
    return ((x32 - mu) * lax.rsqrt(var + LN_EPS) * w + b).astype(x.dtype)


def head_norm(y, eps):
    y = y.astype(jnp.float32)
    mu = jnp.mean(y, -1, keepdims=True)
    var = jnp.mean(jnp.square(y - mu), -1, keepdims=True)
    return (y - mu) * lax.rsqrt(var + eps)


def head_rms(y, gain):
    y = y.astype(jnp.float32)
    return y * lax.rsqrt(jnp.mean(jnp.square(y), -1, keepdims=True) + HEAD_EPS) * gain


def rotary(x):
    S, d = x.shape[1], x.shape[-1]
    theta = 1.0 / (RET_ROPE_BASE ** jnp.linspace(0.0, 1.0, d // 2))
    ang = jnp.arange(S, dtype=jnp.float32)[:, None] * theta[None, :]
    cos, sin = jnp.cos(ang)[:, None, :], jnp.sin(ang)[:, None, :]
    xr = x.astype(jnp.float32).reshape(*x.shape[:-1], d // 2, 2)
    x1, x2 = xr[..., 0], xr[..., 1]
    return jnp.stack([x1 * cos - x2 * sin, x1 * sin + x2 * cos], -1).reshape(x.shape)


def to_chunks(a):
    B, H, T, d = a.shape
    return a.reshape(B, H, T // CHUNK, CHUNK, d).transpose(2, 0, 1, 3, 4)


def from_chunks(o):
    n, B, H, C, d = o.shape
    return o.transpose(1, 2, 0, 3, 4).reshape(B, H, n * C, d)


def retention_chunkwise(q, k, v, log_gamma):
    B, H, T, dk = q.shape
    dv = v.shape[-1]
    pos = jnp.arange(CHUNK, dtype=jnp.float32)
    causal = pos[:, None] >= pos[None, :]
    rel = jnp.where(causal, pos[:, None] - pos[None, :], 0.0)
    decay = jnp.where(causal[None], jnp.exp(log_gamma[:, None, None] * rel[None]), 0.0)
    xi = jnp.exp(log_gamma[:, None] * (pos + 1.0))[None, :, :, None]
    zeta = jnp.exp(log_gamma[:, None] * (CHUNK - 1.0 - pos))[None, :, :, None]
    gamma_c = jnp.exp(log_gamma * CHUNK)[None, :, None, None]

    def step(R, inp):
        qc, kc, vc = inp
        scores = jnp.einsum('bhtd,bhsd->bhts', qc, kc) * decay[None]
        o = jnp.einsum('bhts,bhse->bhte', scores, vc) + jnp.einsum('bhtd,bhde->bhte', qc, R) * xi
        R = gamma_c * R + jnp.einsum('bhsd,bhse->bhde', kc, vc * zeta)
        return R, o

    R0 = jnp.zeros((B, H, dk, dv), jnp.float32)
    _, o = lax.scan(step, R0, (to_chunks(q), to_chunks(k), to_chunks(v)))
    return from_chunks(o)


def chunk_gated_linear_attention(q, k, v, log_g):
    B, H, T, dk = q.shape
    dv = v.shape[-1]
    causal = jnp.tril(jnp.ones((CHUNK, CHUNK), bool))[:, :, None]

    def step(S, inp):
        qc, kc, vc, gc = inp
        b = jnp.cumsum(gc, axis=2)
        diff = b[:, :, :, None, :] - b[:, :, None, :, :]
        decay = jnp.where(causal, jnp.exp(jnp.where(causal, diff, 0.0)), 0.0)
        scores = jnp.einsum('bhtd,bhsd,bhtsd->bhts', qc, kc, decay)
        o = jnp.einsum('bhts,bhse->bhte', scores, vc) + jnp.einsum('bhtd,bhde->bhte', qc * jnp.exp(b), S)
        b_last = b[:, :, -1:, :]
        S = jnp.exp(b_last[:, :, 0, :, None]) * S + jnp.einsum('bhsd,bhse->bhde', kc * jnp.exp(b_last - b), vc)
        return S, o

    S0 = jnp.zeros((B, H, dk, dv), jnp.float32)
    _, o = lax.scan(step, S0, (to_chunks(q), to_chunks(k), to_chunks(v), to_chunks(log_g)))
    return from_chunks(o)


def rwkv7_recurrence(r, w, k, v, kk, a):
    B, S, H, N = r.shape
    seq = tuple(t.transpose(1, 0, 2, 3) for t in (r, w, k, v, kk, a))

    def step(state, inp):
        rt, wt, kt, vt, kkt, at = inp
        sa = jnp.einsum('bhvk,bhk->bhv', state, kkt)
        state = state * wt[:, :, None, :] - sa[..., None] * (kkt * at)[:, :, None, :] + vt[..., None] * kt[:, :, None, :]
        return state, jnp.einsum('bhvk,bhk->bhv', state, rt)

    _, y = lax.scan(step, jnp.zeros((B, H, N, N), jnp.float32), seq)
    return y.transpose(1, 0, 2, 3)


def retnet_branch(z):
    B, S, _ = z.shape
    q, k, v, g = split_cols(z.astype(jnp.float32), RET_COLS)
    q = rotary(heads(q))
    k = rotary(heads(k)) * RET_DK ** -0.5
    log_gamma = jnp.log1p(-jnp.exp2(-5.0 - jnp.arange(HEADS, dtype=jnp.float32)))
    o = retention_chunkwise(bhsd(q), bhsd(k), bhsd(heads(v)), log_gamma)
    o = head_norm(bhsd(o), HEAD_EPS).reshape(B, S, BRANCH_WIDTH)
    return (jax.nn.silu(g) * o).astype(z.dtype)


def rwkv7_branch(z, mu, w0, w2, a0, a2, g2, k_k, k_a, r_k, ln_w, ln_b):
    B, S, _ = z.shape
    z = z.astype(jnp.float32)
    z = z + (shift_right(z) - z) * mu
    r, k, v, wl, al, gl = split_cols(z, RWKV_COLS)
    w = jnp.exp(-RWKV_DECAY_SCALE * jax.nn.sigmoid(w0 + jnp.tanh(wl) @ w2))
    a = jax.nn.sigmoid(a0 + al @ a2)
    g = jax.nn.sigmoid(gl) @ g2
    kk = heads(k * k_k)
    kk = kk * lax.rsqrt(jnp.sum(jnp.square(kk), -1, keepdims=True) + 1e-12)
    k = k * (1.0 + (a - 1.0) * k_a)
    rh, kh, vh = heads(r), heads(k), heads(v)
    y = rwkv7_recurrence(rh, heads(w), kh, vh, kk, heads(a))
    y = head_norm(y, RWKV_GN_EPS).reshape(B, S, BRANCH_WIDTH) * ln_w + ln_b
    bonus = (jnp.sum(rh * kh * r_k, -1, keepdims=True) * vh).reshape(B, S, BRANCH_WIDTH)
    return ((y + bonus) * g).astype(z.dtype)


def gla_branch(z, w2, b, norm_w):
    B, S, _ = z.shape
    q, k, v, gl, g = split_cols(z.astype(jnp.float32), GLA_COLS)
    log_alpha = jax.nn.log_sigmoid(gl @ w2 + b) / GLA_GATE_TEMP
    o = chunk_gated_linear_attention(bhsd(heads(q) * GLA_DK ** -0.5), bhsd(heads(k)), bhsd(heads(v)), bhsd(heads(log_alpha)))
    o = head_rms(bhsd(o), norm_w).reshape(B, S, BRANCH_WIDTH)
    return (o * jax.nn.silu(g)).astype(z.dtype)


def hgrn2_branch(z, lb, norm_w):
    B, S, _ = z.shape
    q, fz, i, g = split_cols(z.astype(jnp.float32), HGRN_COLS)
    f = lb + (1.0 - lb) * jax.nn.sigmoid(fz)
    log_f = jnp.log(f)
    k = (1.0 - lb) * jax.nn.sigmoid(-fz)
    o = chunk_gated_linear_attention(bhsd(heads(q)), bhsd(heads(k)), bhsd(heads(i)), bhsd(heads(log_f)))
    o = head_rms(bhsd(o), norm_w).reshape(B, S, BRANCH_WIDTH)
    return (o * jax.nn.silu(g)).astype(z.dtype)


def conv_ffn(x, w_up, conv_w, w_down):
    S = x.shape[1]
    u, v = jnp.split(x @ w_up, 2, axis=-1)
    up = jnp.pad(u, ((0, 0), (FFN_CONV_WIDTH - 1, 0), (0, 0)))
    u = sum(conv_w[j] * up[:, j:j + S] for j in range(FFN_CONV_WIDTH))
    return (jax.nn.gelu(u) * v) @ w_down


def _normal(k, shape, scale):
    return scale * jax.random.normal(k, shape, jnp.float32)


def setup_inputs(seed: int = 0) -> dict:
    key = jax.random.key(seed)
    k = jax.random.split(key, 32)
    L, D, W = DEPTH, D_MODEL, BRANCH_WIDTH
    return {
        'x': _normal(k[0], (BATCH, SEQ, D), 1.0),
        'p': _normal(k[1], (DEPTH, BATCH, SEQ, PLE_DIM), 1.0),
        'ln_in_w': 1.0 + _normal(k[2], (D,), 0.02),
        'ln_in_b': _normal(k[3], (D,), 0.02),
        'w_in': _normal(k[4], (L, D, N_IN), D ** -0.5),
        'rwkv_mu': jax.random.uniform(k[5], (L, sum(RWKV_COLS)), jnp.float32),
        'rwkv_w0': _normal(k[6], (L, W), 0.5),
        'rwkv_w2': _normal(k[7], (L, RWKV_W_LORA, W), 0.5 * RWKV_W_LORA ** -0.5),
        'rwkv_a0': _normal(k[8], (L, W), 0.1),
        'rwkv_a2': _normal(k[9], (L, RWKV_A_LORA, W), 0.5 * RWKV_A_LORA ** -0.5),
        'rwkv_g2': _normal(k[10], (L, RWKV_G_LORA, W), RWKV_G_LORA ** -0.5),
        'rwkv_k_k': 0.85 + _normal(k[11], (L, W), 0.05),
        'rwkv_k_a': 1.0 + _normal(k[12], (L, W), 0.05),
        'rwkv_r_k': _normal(k[13], (L, HEADS, RWKV_N), 0.1),
        'rwkv_ln_w': 1.0 + _normal(k[14], (L, W), 0.02),
        'rwkv_ln_b': _normal(k[15], (L, W), 0.02),
        'gla_w2': _normal(k[16], (L, GLA_GATE_LORA, HEADS * GLA_DK), GLA_GATE_LORA ** -0.5),
        'gla_b': _normal(k[17], (L, HEADS * GLA_DK), 0.1),
        'gla_norm_w': 1.0 + _normal(k[18], (L, HEAD_DV), 0.02),
        'hgrn_lb_logits': _normal(k[19], (L, HEADS * HGRN_DK), 0.1),
        'hgrn_norm_w': 1.0 + _normal(k[20], (L, HEAD_DV), 0.02),
        'w_branch': _normal(k[21], (L, N_BRANCH, W, D), W ** -0.5),
        'w_mix_out': _normal(k[22], (L, D, D), BETA * D ** -0.5),
        'ln_mix_w': 1.0 + _normal(k[23], (L, D), 0.02),
        'ln_mix_b': _normal(k[24], (L, D), 0.02),
        'w_ffn_up': _normal(k[25], (L, D, 2 * D_FF), D ** -0.5),
        'ffn_conv': _normal(k[26], (L, FFN_CONV_WIDTH, D_FF), FFN_CONV_WIDTH ** -0.5),
        'w_ffn_down': _normal(k[27], (L, D_FF, D), BETA * D_FF ** -0.5),
        'w_ple_gate': _normal(k[28], (L, D, D), D ** -0.5),
        'w_ple_proj': _normal(k[29], (L, PLE_DIM, D), BETA * PLE_DIM ** -0.5),
        'ln_ffn_w': 1.0 + _normal(k[30], (L, D), 0.02),
        'ln_ffn_b': _normal(k[31], (L, D), 0.02),
    }


def reference(x, p, ln_in_w, ln_in_b, w_in, rwkv_mu, rwkv_w0, rwkv_w2, rwkv_a0, rwkv_a2, rwkv_g2,
              rwkv_k_k, rwkv_k_a, rwkv_r_k, rwkv_ln_w, rwkv_ln_b, gla_w2, gla_b, gla_norm_w,
              hgrn_lb_logits, hgrn_norm_w, w_branch, w_mix_out, ln_mix_w, ln_mix_b,
              w_ffn_up, ffn_conv, w_ffn_down, w_ple_gate, w_ple_proj, ln_ffn_w, ln_ffn_b):
    lb_prob = jax.nn.softmax(hgrn_lb_logits.astype(jnp.float32), axis=0)
    lb_all = jnp.concatenate([jnp.zeros_like(lb_prob[:1]), jnp.cumsum(lb_prob[1:], axis=0)], axis=0)

    x = layer_norm(x, ln_in_w, ln_in_b)
    for i in range(DEPTH):
        z = x @ w_in[i]
        z_ret, z_rwkv, z_gla, z_hgrn, z_gate = split_cols(z, GROUP_COLS)
        branches = (
            retnet_branch(z_ret),
            rwkv7_branch(z_rwkv, rwkv_mu[i], rwkv_w0[i], rwkv_w2[i], rwkv_a0[i], rwkv_a2[i], rwkv_g2[i],
                         rwkv_k_k[i], rwkv_k_a[i], rwkv_r_k[i], rwkv_ln_w[i], rwkv_ln_b[i]),
            gla_branch(z_gla, gla_w2[i], gla_b[i], gla_norm_w[i]),
            hgrn2_branch(z_hgrn, lb_all[i], hgrn_norm_w[i]),
        )
        merged = jnp.zeros_like(x)
        for n in range(N_BRANCH):
            gate = jax.nn.sigmoid(z_gate[..., n * D_MODEL:(n + 1) * D_MODEL])
            merged = merged + gate * (branches[n] @ w_branch[i, n])
        x = layer_norm(ALPHA * x + merged @ w_mix_out[i], ln_mix_w[i], ln_mix_b[i])
        ffn = conv_ffn(x, w_ffn_up[i], ffn_conv[i], w_ffn_down[i])
        ple = jax.nn.sigmoid(x @ w_ple_gate[i]) * (p[i] @ w_ple_proj[i])
        x = layer_norm(ALPHA * x + ffn + ple, ln_ffn_w[i], ln_ffn_b[i])
    return x
```

```python
import functools
import math

import numpy as np
import jax
import jax.numpy as jnp
from jax import lax
from jax.experimental import pallas as pl
from jax.experimental.pallas import tpu as pltpu

F32 = jnp.float32
BF16 = jnp.bfloat16

D_MODEL = 1024
DEPTH = 2
N_BRANCH = 4
HEADS = 4
BRANCH_WIDTH = D_MODEL // 4
HEAD_DV = BRANCH_WIDTH // HEADS
RET_DK = HEAD_DV // 2
GLA_DK = HEAD_DV // 2
HGRN_DK = HEAD_DV
RWKV_N = HEAD_DV
RWKV_W_LORA = 64
RWKV_A_LORA = 64
RWKV_G_LORA = 128
GLA_GATE_LORA = 16
GLA_GATE_TEMP = 16.0
RET_ROPE_BASE = 10000.0
RWKV_DECAY_SCALE = 0.6065306597126334
D_FF = 2816
PLE_DIM = 256
LN_EPS = 1e-5
HEAD_EPS = 1e-6
RWKV_GN_EPS = 64e-5
ALPHA = (2.0 * DEPTH) ** 0.25

RET_COLS = (HEADS * RET_DK, HEADS * RET_DK, BRANCH_WIDTH, BRANCH_WIDTH)
RWKV_COLS = (BRANCH_WIDTH, BRANCH_WIDTH, BRANCH_WIDTH, RWKV_W_LORA, RWKV_A_LORA, RWKV_G_LORA)
GLA_COLS = (HEADS * GLA_DK, HEADS * GLA_DK, BRANCH_WIDTH, GLA_GATE_LORA, BRANCH_WIDTH)
HGRN_COLS = (HEADS * HGRN_DK, HEADS * HGRN_DK, BRANCH_WIDTH, BRANCH_WIDTH)
GROUP_COLS = (sum(RET_COLS), sum(RWKV_COLS), sum(GLA_COLS), sum(HGRN_COLS), N_BRANCH * D_MODEL)

V7X_VMEM_LIMIT_BYTES = 56 * 1024 * 1024
LANES = 128

TIME_TILE = 256
RET_CHUNK = 128
GATED_CHUNK = 64
GATED_SUB = 16
RWKV_CHUNK = 64
ROW_TILE = 512


_NN = (((1,), (0,)), ((), ()))
_NT = (((1,), (1,)), ((), ()))
_TN = (((0,), (0,)), ((), ()))


def _split(x, n):
    if x.dtype == BF16:
        return [x]
    parts = []
    r = x
    for i in range(n):
        p = r.astype(BF16)
        parts.append(p)
        if i + 1 < n:
            r = r - p.astype(F32)
    return parts


def _mm(a, b, dims=_NN, na=1, nb=1):
    ap = _split(a, na)
    bp = _split(b, nb)
    order = max(len(ap), len(bp))
    acc = None
    for i, x in enumerate(ap):
        for j, y in enumerate(bp):
            if i + j >= order:
                continue
            t = lax.dot_general(x, y, dims, preferred_element_type=F32)
            acc = t if acc is None else acc + t
    return acc


def _sigmoid(x):
    return 1.0 / (1.0 + jnp.exp(-x))


def _silu(x):
    return x * _sigmoid(x)


def _log_sigmoid(x):
    return jnp.minimum(x, 0.0) - jnp.log1p(jnp.exp(-jnp.abs(x)))


def _gelu_tanh(x):
    return 0.5 * x * (1.0 + jnp.tanh(math.sqrt(2.0 / math.pi) * (x + 0.044715 * (x * x * x))))


def _layer_norm_rows(h, w, b):
    mu = jnp.mean(h, axis=-1, keepdims=True)
    hc = h - mu
    var = jnp.mean(hc * hc, axis=-1, keepdims=True)
    return hc * lax.rsqrt(var + LN_EPS) * w + b


def _head_stack(x, head_of_lane):
    return jnp.concatenate([jnp.where(head_of_lane == h, x, 0.0) for h in range(HEADS)], axis=0)


def _head_unstack(y, head_of_lane, c):
    acc = None
    for h in range(HEADS):
        t = jnp.where(head_of_lane == h, y[h * c:(h + 1) * c], 0.0)
        acc = t if acc is None else acc + t
    return acc


def _seg_mean(x, avg):
    return _mm(x, avg, na=2, nb=1)


def _lane_heads(width, per_head):
    return lax.broadcasted_iota(jnp.int32, (1, width), 1) // per_head


def _ln_body(x_ref, w_ref, b_ref, o32_ref, o16_ref):
    y = _layer_norm_rows(x_ref[...], w_ref[...], b_ref[...])
    o32_ref[...] = y
    o16_ref[...] = y.astype(BF16)


def _ln_call(x2, w, b):
    t, d = x2.shape
    return pl.pallas_call(
        _ln_body,
        grid=(t // ROW_TILE,),
        in_specs=[
            pl.BlockSpec((ROW_TILE, d), lambda i: (i, 0)),
            pl.BlockSpec((1, d), lambda i: (0, 0)),
            pl.BlockSpec((1, d), lambda i: (0, 0)),
        ],
        out_specs=[
            pl.BlockSpec((ROW_TILE, d), lambda i: (i, 0)),
            pl.BlockSpec((ROW_TILE, d), lambda i: (i, 0)),
        ],
        out_shape=[jax.ShapeDtypeStruct((t, d), F32), jax.ShapeDtypeStruct((t, d), BF16)],
        compiler_params=pltpu.CompilerParams(dimension_semantics=("parallel",)),
        name="ln_in",
    )(x2, w.reshape(1, d), b.reshape(1, d))


def _ret_body(x_ref, w_ref, cos_ref, sin_ref, dm_ref, xi_ref, zeta_ref, gc_ref, bd_ref, avg_ref,
              o_ref, z_scr, st_ref, *, ts, c):
    @pl.when(pl.program_id(1) == 0)
    def _():
        st_ref[...] = jnp.zeros_like(st_ref)

    z_scr[...] = jnp.dot(x_ref[0], w_ref[...], preferred_element_type=F32)
    hq = (lax.broadcasted_iota(jnp.int32, (1, 128), 1) % 64) // (RET_DK // 2)
    hv = _lane_heads(BRANCH_WIDTH, HEAD_DV)

    def chunk(ci, carry):
        r0 = pl.multiple_of(ci * c, c)
        rows = pl.ds(r0, c)
        zq = z_scr[rows, 0:128]
        zk = z_scr[rows, 128:256]
        v = z_scr[rows, 256:512]
        g = z_scr[rows, 512:768]
        cs = cos_ref[rows, :]
        sn = sin_ref[rows, :]
        q = zq * cs + pltpu.roll(zq, 64, 1) * sn
        k = (zk * cs + pltpu.roll(zk, 64, 1) * sn) * (RET_DK ** -0.5)
        sc = _mm(_head_stack(q, hq), k, _NT)
        p = sc * dm_ref[...]
        inner = _head_unstack(_mm(p, v), hv, c)
        r_state = st_ref[...]
        cross = _mm(q * xi_ref[...], r_state)
        o = inner + cross
        st_ref[...] = r_state * gc_ref[...] + _mm(k, v * zeta_ref[...], _TN) * bd_ref[...]
        mean = _seg_mean(o, avg_ref[...])
        d = o - mean
        var = _seg_mean(d * d, avg_ref[...])
        y = d * lax.rsqrt(var + HEAD_EPS)
        o_ref[0, rows, :] = (_silu(g) * y).astype(o_ref.dtype)
        return carry

    lax.fori_loop(0, ts // c, chunk, 0)


def _const_spec(shape):
    nd = len(shape)
    return pl.BlockSpec(shape, lambda b, t: (0,) * nd)


def _ret_tables(c):
    hs = np.arange(HEADS, dtype=np.float32)
    log_gamma = jnp.log1p(-jnp.exp2(-5.0 - jnp.asarray(hs)))
    pos = jnp.arange(c, dtype=F32)
    causal = pos[:, None] >= pos[None, :]
    rel = jnp.where(causal, pos[:, None] - pos[None, :], 0.0)
    decay = jnp.where(causal[None], jnp.exp(log_gamma[:, None, None] * rel[None]), 0.0)
    dm = decay.reshape(HEADS * c, c)
    q_head = (np.arange(128) % 64) // (RET_DK // 2)
    v_head = np.arange(BRANCH_WIDTH) // HEAD_DV
    xi = jnp.exp(log_gamma[None, :] * (pos[:, None] + 1.0))[:, q_head]
    zeta = jnp.exp(log_gamma[None, :] * (c - 1.0 - pos[:, None]))[:, v_head]
    gc = jnp.exp(log_gamma * c)[v_head][None, :]
    bd = jnp.asarray((q_head[:, None] == v_head[None, :]).astype(np.float32))
    return dm, xi, zeta, gc, bd


def _rope_tables(s):
    half = RET_DK // 2
    theta = 1.0 / (RET_ROPE_BASE ** jnp.linspace(0.0, 1.0, half))
    ang = jnp.arange(s, dtype=F32)[:, None] * theta[None, :]
    idx = np.arange(128) % half
    cos = jnp.cos(ang)[:, idx]
    sin = jnp.sin(ang)[:, idx]
    sign = jnp.asarray(np.where(np.arange(128) < 64, -1.0, 1.0).astype(np.float32))
    return cos, sin * sign[None, :]


def _ret_perm():
    half = RET_DK // 2
    l = np.arange(128)
    h = (l % 64) // half
    i = l % half
    return h * RET_DK + 2 * i + (l >= 64)


def _ret_call(xb, w, s_tables, avg):
    b, s, d = xb.shape
    ts, c = TIME_TILE, RET_CHUNK
    cos, sin = s_tables
    dm, xi, zeta, gc, bd = _ret_tables(c)
    body = functools.partial(_ret_body, ts=ts, c=c)
    return pl.pallas_call(
        body,
        grid=(b, s // ts),
        in_specs=[
            pl.BlockSpec((1, ts, d), lambda i, t: (i, t, 0)),
            _const_spec(w.shape),
            pl.BlockSpec((ts, 128), lambda i, t: (t, 0)),
            pl.BlockSpec((ts, 128), lambda i, t: (t, 0)),
            _const_spec(dm.shape), _const_spec(xi.shape), _const_spec(zeta.shape),
            _const_spec(gc.shape), _const_spec(bd.shape), _const_spec(avg.shape),
        ],
        out_specs=pl.BlockSpec((1, ts, BRANCH_WIDTH), lambda i, t: (i, t, 0)),
        out_shape=jax.ShapeDtypeStruct((b, s, BRANCH_WIDTH), BF16),
        scratch_shapes=[pltpu.VMEM((ts, w.shape[1]), F32), pltpu.VMEM((128, BRANCH_WIDTH), F32)],
        compiler_params=pltpu.CompilerParams(dimension_semantics=("parallel", "arbitrary"),
                                             vmem_limit_bytes=V7X_VMEM_LIMIT_BYTES),
        name="retention",
    )(xb, w, cos, sin, dm, xi, zeta, gc, bd, avg)


def _gated_chunk(q, k, v, glog, st_ref, tri, hexp, bdt, hk, hv, c, sub):
    w = q.shape[1]
    nb = c // sub
    bcum = _mm(tri, glog, na=1, nb=3)
    st = st_ref[...]
    cross = _mm(q * jnp.exp(bcum), st, _NT)
    blast = bcum[c - 1:c, :]
    kd = k * jnp.exp(blast - bcum)
    st_ref[...] = st * jnp.exp(blast) + _mm(v, kd, _TN) * bdt

    row = lax.broadcasted_iota(jnp.int32, (c, 1), 0)
    trow = lax.broadcasted_iota(jnp.int32, (sub, 1), 0)
    outs = []
    for i in range(nb):
        lo, hi = i * sub, (i + 1) * sub
        bi, qi, ki, vi = bcum[lo:hi], q[lo:hi], k[lo:hi], v[lo:hi]
        acc = cross[lo:hi]
        if i > 0:
            ref = bcum[lo - 1:lo, :]
            qt = qi * jnp.exp(bi - ref)
            kt = jnp.where(row < lo, k * jnp.exp(jnp.minimum(ref - bcum, 0.0)), 0.0)
            sc = _mm(_head_stack(qt, hk), kt, _NT)
            acc = acc + _head_unstack(_mm(sc, v), hv, sub)
        ps = []
        for s in range(sub):
            bs = jnp.broadcast_to(bi[s:s + 1, :], (sub, w))
            ks = jnp.broadcast_to(ki[s:s + 1, :], (sub, w))
            e = jnp.exp(jnp.minimum(bi - bs, 0.0))
            ps.append(jnp.where(trow >= s, qi * ks * e, 0.0))
        rep = _mm(jnp.concatenate(ps, axis=0), hexp)
        for s in range(sub):
            acc = acc + rep[s * sub:(s + 1) * sub] * jnp.broadcast_to(vi[s:s + 1, :], (sub, BRANCH_WIDTH))
        outs.append(acc)
    return jnp.concatenate(outs, axis=0)


def _gla_body(x_ref, w_ref, w2_ref, b_ref, nw_ref, tri_ref, hexp_ref, bdt_ref, avg_ref,
              o_ref, z_scr, st_ref, *, ts, c, sub):
    @pl.when(pl.program_id(1) == 0)
    def _():
        st_ref[...] = jnp.zeros_like(st_ref)

    z_scr[...] = jnp.dot(x_ref[0], w_ref[...], preferred_element_type=F32)
    hk = _lane_heads(HEADS * GLA_DK, GLA_DK)
    hv = _lane_heads(BRANCH_WIDTH, HEAD_DV)

    def chunk(ci, carry):
        r0 = pl.multiple_of(ci * c, c)
        rows = pl.ds(r0, c)
        q = z_scr[rows, 0:128] * (GLA_DK ** -0.5)
        k = z_scr[rows, 128:256]
        v = z_scr[rows, 256:512]
        g = z_scr[rows, 512:768]
        gl = z_scr[rows, 768:896]
        glog = _log_sigmoid(_mm(gl, w2_ref[...], na=2, nb=2) + b_ref[...]) / GLA_GATE_TEMP
        o = _gated_chunk(q, k, v, glog, st_ref, tri_ref[...], hexp_ref[...], bdt_ref[...], hk, hv, c, sub)
        ms = _seg_mean(o * o, avg_ref[...])
        y = o * lax.rsqrt(ms + HEAD_EPS) * nw_ref[...]
        o_ref[0, rows, :] = (y * _silu(g)).astype(o_ref.dtype)
        return carry

    lax.fori_loop(0, ts // c, chunk, 0)


def _hgrn_body(x_ref, w_ref, lbl_ref, nw_ref, tri_ref, hexp_ref, bdt_ref, avg_ref,
               o_ref, z_scr, st_ref, *, ts, c, sub, layer):
    @pl.when(pl.program_id(1) == 0)
    def _():
        st_ref[...] = jnp.zeros_like(st_ref)

    z_scr[...] = jnp.dot(x_ref[0], w_ref[...], preferred_element_type=F32)
    hk = _lane_heads(HEADS * HGRN_DK, HGRN_DK)
    hv = _lane_heads(BRANCH_WIDTH, HEAD_DV)
    logits = lbl_ref[...]
    ex = jnp.exp(logits - jnp.max(logits, axis=0, keepdims=True))
    prob = ex / jnp.sum(ex, axis=0, keepdims=True)
    lb = jnp.zeros((1, HEADS * HGRN_DK), F32)
    for j in range(1, layer + 1):
        lb = lb + prob[j:j + 1, :]

    def chunk(ci, carry):
        r0 = pl.multiple_of(ci * c, c)
        rows = pl.ds(r0, c)
        q = z_scr[rows, 0:256]
        fz = z_scr[rows, 256:512]
        v = z_scr[rows, 512:768]
        g = z_scr[rows, 768:1024]
        f = lb + (1.0 - lb) * _sigmoid(fz)
        glog = jnp.log(f)
        k = (1.0 - lb) * _sigmoid(-fz)
        o = _gated_chunk(q, k, v, glog, st_ref, tri_ref[...], hexp_ref[...], bdt_ref[...], hk, hv, c, sub)
        ms = _seg_mean(o * o, avg_ref[...])
        y = o * lax.rsqrt(ms + HEAD_EPS) * nw_ref[...]
        o_ref[0, rows, :] = (y * _silu(g)).astype(o_ref.dtype)
        return carry

    lax.fori_loop(0, ts // c, chunk, 0)


def _gated_consts(wk, dk, c):
    tri = jnp.asarray(np.tril(np.ones((c, c), np.float32))).astype(BF16)
    k_head = np.arange(wk) // dk
    v_head = np.arange(BRANCH_WIDTH) // HEAD_DV
    same = (k_head[:, None] == v_head[None, :]).astype(np.float32)
    hexp = jnp.asarray(same).astype(BF16)
    bdt = jnp.asarray(same.T)
    return tri, hexp, bdt


def _gla_call(xb, w, w2p, bias, norm_w, avg):
    b, s, d = xb.shape
    ts, c, sub = TIME_TILE, GATED_CHUNK, GATED_SUB
    wk = HEADS * GLA_DK
    tri, hexp, bdt = _gated_consts(wk, GLA_DK, c)
    body = functools.partial(_gla_body, ts=ts, c=c, sub=sub)
    return pl.pallas_call(
        body,
        grid=(b, s // ts),
        in_specs=[
            pl.BlockSpec((1, ts, d), lambda i, t: (i, t, 0)),
            _const_spec(w.shape), _const_spec(w2p.shape), _const_spec(bias.shape), _const_spec(norm_w.shape),
            _const_spec(tri.shape), _const_spec(hexp.shape), _const_spec(bdt.shape), _const_spec(avg.shape),
        ],
        out_specs=pl.BlockSpec((1, ts, BRANCH_WIDTH), lambda i, t: (i, t, 0)),
        out_shape=jax.ShapeDtypeStruct((b, s, BRANCH_WIDTH), BF16),
        scratch_shapes=[pltpu.VMEM((ts, w.shape[1]), F32), pltpu.VMEM((BRANCH_WIDTH, wk), F32)],
        compiler_params=pltpu.CompilerParams(dimension_semantics=("parallel", "arbitrary"),
                                             vmem_limit_bytes=V7X_VMEM_LIMIT_BYTES),
        name="gla",
    )(xb, w, w2p, bias, norm_w, tri, hexp, bdt, avg)


def _hgrn_call(xb, w, lb_logits, norm_w, avg, layer):
    b, s, d = xb.shape
    ts, c, sub = TIME_TILE, GATED_CHUNK, GATED_SUB
    wk = HEADS * HGRN_DK
    tri, hexp, bdt = _gated_consts(wk, HGRN_DK, c)
    body = functools.partial(_hgrn_body, ts=ts, c=c, sub=sub, layer=layer)
    return pl.pallas_call(
        body,
        grid=(b, s // ts),
        in_specs=[
            pl.BlockSpec((1, ts, d), lambda i, t: (i, t, 0)),
            _const_spec(w.shape), _const_spec(lb_logits.shape), _const_spec(norm_w.shape),
            _const_spec(tri.shape), _const_spec(hexp.shape), _const_spec(bdt.shape), _const_spec(avg.shape),
        ],
        out_specs=pl.BlockSpec((1, ts, BRANCH_WIDTH), lambda i, t: (i, t, 0)),
        out_shape=jax.ShapeDtypeStruct((b, s, BRANCH_WIDTH), BF16),
        scratch_shapes=[pltpu.VMEM((ts, w.shape[1]), F32), pltpu.VMEM((BRANCH_WIDTH, wk), F32)],
        compiler_params=pltpu.CompilerParams(dimension_semantics=("parallel", "arbitrary"),
                                             vmem_limit_bytes=V7X_VMEM_LIMIT_BYTES),
        name="hgrn2",
    )(xb, w, lb_logits, norm_w, tri, hexp, bdt, avg)


RWKV_NP = 2


def _rwkv_body(x_ref, w_ref, mu_ref, w0_ref, w2_ref, a0_ref, a2_ref, g2_ref, kk_ref, ka_ref, rk_ref,
               lnw_ref, lnb_ref, tri_ref, ms_ref, mi_ref, eye_ref, avg_ref,
               o_ref, z_scr, st_ref, last_ref, *, ts, c):
    @pl.when(pl.program_id(1) == 0)
    def _():
        st_ref[...] = jnp.zeros_like(st_ref)
        last_ref[...] = jnp.zeros_like(last_ref)

    z = jnp.dot(x_ref[0], w_ref[...], preferred_element_type=F32)
    rowi = lax.broadcasted_iota(jnp.int32, (ts, 1), 0)
    prev = jnp.where(rowi == 0, jnp.broadcast_to(last_ref[7:8, :], z.shape), pltpu.roll(z, 1, 0))
    last_ref[...] = z[ts - 8:ts, :]
    z_scr[...] = z + (prev - z) * mu_ref[...]

    hl = _lane_heads(BRANCH_WIDTH, RWKV_N)
    npc = RWKV_NP
    mm = functools.partial(_mm, na=npc, nb=npc)

    def chunk(ci, carry):
        r0 = pl.multiple_of(ci * c, c)
        rows = pl.ds(r0, c)
        r = z_scr[rows, 0:256]
        k = z_scr[rows, 256:512]
        v = z_scr[rows, 512:768]
        wl = z_scr[rows, 768:832]
        al = z_scr[rows, 832:896]
        gl = z_scr[rows, 896:1024]
        lw = -RWKV_DECAY_SCALE * _sigmoid(w0_ref[...] + _mm(jnp.tanh(wl), w2_ref[...], na=2, nb=2))
        a = _sigmoid(a0_ref[...] + _mm(al, a2_ref[...], na=2, nb=2))
        g = _mm(_sigmoid(gl), g2_ref[...], na=2, nb=2)
        kk = k * kk_ref[...]
        kk = kk * lax.rsqrt(_mm(kk * kk, avg_ref[...], na=2, nb=1) * float(RWKV_N) + 1e-12)
        k2 = k * (1.0 + (a - 1.0) * ka_ref[...])
        beta = kk * a
        cw = _mm(tri_ref[...], lw, na=1, nb=3)
        e_in = jnp.exp(cw)
        e_out = jnp.exp(-cw)
        at = -kk * jnp.exp(cw - lw)
        rt = r * e_in
        bt = beta * e_out
        kt = k2 * e_out
        hs_a = _head_stack(at, hl)
        hs_r = _head_stack(rt, hl)
        t4b = jnp.concatenate([bt] * HEADS, axis=0)
        t4k = jnp.concatenate([kt] * HEADS, axis=0)
        ms = ms_ref[...] > 0.5
        mi = mi_ref[...] > 0.5
        a_ab = jnp.where(ms, mm(hs_a, t4b, _NT), 0.0)
        a_ak = jnp.where(ms, mm(hs_a, t4k, _NT), 0.0)
        a_rb = jnp.where(mi, mm(hs_r, t4b, _NT), 0.0)
        a_rk = jnp.where(mi, mm(hs_r, t4k, _NT), 0.0)
        tinv = eye_ref[...] + a_ab
        pw = a_ab
        for _ in range(int(math.log2(c)) - 1):
            pw = mm(pw, pw)
            tinv = tinv + mm(tinv, pw)
        v_st = _head_stack(v, hl)
        w1 = mm(tinv, mm(a_ak, v_st))
        w2 = mm(tinv, hs_a)
        st = st_ref[...]
        y_st = mm(hs_r + mm(a_rb, w2), st) + mm(a_rb, w1) + mm(a_rk, v_st)
        y = y_st[0:c] + y_st[c:2 * c] + y_st[2 * c:3 * c] + y_st[3 * c:4 * c]
        cwl = cw[c - 1:c, :]
        dec = jnp.exp(cwl - cw)
        hs_bh = _head_stack(beta * dec, hl)
        hs_kh = _head_stack(k2 * dec, hl)
        m_t = eye_ref[...] * jnp.exp(cwl) + mm(hs_bh, w2, _TN)
        st_ref[...] = mm(m_t, st) + mm(hs_bh, w1, _TN) + mm(hs_kh, v_st, _TN)
        mean = _seg_mean(y, avg_ref[...])
        d = y - mean
        var = _seg_mean(d * d, avg_ref[...])
        yn = d * lax.rsqrt(var + RWKV_GN_EPS) * lnw_ref[...] + lnb_ref[...]
        bonus = _mm(r * k2 * rk_ref[...], avg_ref[...], na=2, nb=1) * float(RWKV_N) * v
        o_ref[0, rows, :] = ((yn + bonus) * g).astype(o_ref.dtype)
        return carry

    lax.fori_loop(0, ts // c, chunk, 0)


def _rwkv_call(xb, w, mu, w0, w2, a0, a2, g2, k_k, k_a, r_k, ln_w, ln_b, avg):
    b, s, d = xb.shape
    ts, c = TIME_TILE, RWKV_CHUNK
    n = HEADS * c
    tri = jnp.asarray(np.tril(np.ones((c, c), np.float32))).astype(BF16)
    ridx = np.arange(n)
    same = (ridx[:, None] // c) == (ridx[None, :] // c)
    ms = jnp.asarray((same & ((ridx[None, :] % c) < (ridx[:, None] % c))).astype(np.float32))
    mi = jnp.asarray((same & ((ridx[None, :] % c) <= (ridx[:, None] % c))).astype(np.float32))
    eye = jnp.asarray(np.eye(n, dtype=np.float32))
    row = lambda a: a.reshape(1, -1).astype(F32)
    args = (xb, w, row(mu), row(w0), w2.astype(F32), row(a0), a2.astype(F32), g2.astype(F32),
            row(k_k), row(k_a), row(r_k), row(ln_w), row(ln_b), tri, ms, mi, eye, avg)
    in_specs = [pl.BlockSpec((1, ts, d), lambda i, t: (i, t, 0))] + [_const_spec(a.shape) for a in args[1:]]
    body = functools.partial(_rwkv_body, ts=ts, c=c)
    return pl.pallas_call(
        body,
        grid=(b, s // ts),
        in_specs=in_specs,
        out_specs=pl.BlockSpec((1, ts, BRANCH_WIDTH), lambda i, t: (i, t, 0)),
        out_shape=jax.ShapeDtypeStruct((b, s, BRANCH_WIDTH), BF16),
        scratch_shapes=[pltpu.VMEM((ts, w.shape[1]), F32), pltpu.VMEM((n, BRANCH_WIDTH), F32),
                        pltpu.VMEM((8, w.shape[1]), F32)],
        compiler_params=pltpu.CompilerParams(dimension_semantics=("parallel", "arbitrary"),
                                             vmem_limit_bytes=V7X_VMEM_LIMIT_BYTES),
        name="rwkv7",
    )(*args)


def _merge_body(x32_ref, xb_ref, b0_ref, b1_ref, b2_ref, b3_ref, wg_ref, wb_ref, wm_ref, lw_ref, lb_ref,
                o32_ref, o16_ref):
    xb = xb_ref[...]
    acc = None
    for n, br_ref in enumerate((b0_ref, b1_ref, b2_ref, b3_ref)):
        gate = jnp.dot(xb, wg_ref[:, n * D_MODEL:(n + 1) * D_MODEL], preferred_element_type=F32)
        br = jnp.dot(br_ref[...], wb_ref[n], preferred_element_type=F32)
        t = _sigmoid(gate) * br
        acc = t if acc is None else acc + t
    y = jnp.dot(acc.astype(BF16), wm_ref[...], preferred_element_type=F32)
    out = _layer_norm_rows(ALPHA * x32_ref[...] + y, lw_ref[...], lb_ref[...])
    o32_ref[...] = out
    o16_ref[...] = out.astype(BF16)


def _merge_call(x32, xb, branches, wg, wb, wm, ln_w, ln_b):
    t, d = x32.shape
    tm = ROW_TILE
    row_spec = lambda width: pl.BlockSpec((tm, width), lambda i: (i, 0))
    full = lambda a: pl.BlockSpec(a.shape, lambda i: (0,) * a.ndim)
    lw = ln_w.reshape(1, d)
    lb = ln_b.reshape(1, d)
    return pl.pallas_call(
        _merge_body,
        grid=(t // tm,),
        in_specs=[row_spec(d), row_spec(d)] + [row_spec(BRANCH_WIDTH)] * N_BRANCH
                 + [full(wg), full(wb), full(wm), full(lw), full(lb)],
        out_specs=[row_spec(d), row_spec(d)],
        out_shape=[jax.ShapeDtypeStruct((t, d), F32), jax.ShapeDtypeStruct((t, d), BF16)],
        compiler_params=pltpu.CompilerParams(dimension_semantics=("parallel",),
                                             vmem_limit_bytes=V7X_VMEM_LIMIT_BYTES),
        name="merge",
    )(x32, xb, *branches, wg, wb, wm, lw, lb)


FFN_CHUNK = 1408


def _ffn_body(x32_ref, xb_ref, p_ref, wup_ref, cw_ref, wdn_ref, wpg_ref, wpp_ref, lw_ref, lb_ref,
              o32_ref, o16_ref, carry_ref, *, tm):
    @pl.when(pl.program_id(1) == 0)
    def _():
        carry_ref[...] = jnp.zeros_like(carry_ref)

    xb = xb_ref[0]
    rowi = lax.broadcasted_iota(jnp.int32, (tm, 1), 0)
    acc = None
    for ci in range(D_FF // FFN_CHUNK):
        lo, hi = ci * FFN_CHUNK, (ci + 1) * FFN_CHUNK
        u = jnp.dot(xb, wup_ref[:, lo:hi], preferred_element_type=F32)
        v = jnp.dot(xb, wup_ref[:, D_FF + lo:D_FF + hi], preferred_element_type=F32)
        prev = carry_ref[:, lo:hi]
        p1 = jnp.broadcast_to(prev[7:8, :], u.shape)
        p2 = jnp.broadcast_to(prev[6:7, :], u.shape)
        u1 = jnp.where(rowi == 0, p1, pltpu.roll(u, 1, 0))
        u2 = jnp.where(rowi == 0, p2, jnp.where(rowi == 1, p1, pltpu.roll(u, 2, 0)))
        carry_ref[:, lo:hi] = u[tm - 8:tm, :]
        uc = cw_ref[0:1, lo:hi] * u2 + cw_ref[1:2, lo:hi] * u1 + cw_ref[2:3, lo:hi] * u
        h = (_gelu_tanh(uc) * v).astype(BF16)
        t = jnp.dot(h, wdn_ref[lo:hi, :], preferred_element_type=F32)
        acc = t if acc is None else acc + t
    ple = _sigmoid(jnp.dot(xb, wpg_ref[...], preferred_element_type=F32)) * jnp.dot(
        p_ref[0].astype(BF16), wpp_ref[...], preferred_element_type=F32)
    out = _layer_norm_rows(ALPHA * x32_ref[0] + acc + ple, lw_ref[...], lb_ref[...])
    o32_ref[0] = out
    o16_ref[0] = out.astype(BF16)


def _ffn_call(x32, xb, p, wup, convw, wdn, wpg, wpp, ln_w, ln_b):
    b, s, d = x32.shape
    tm = ROW_TILE
    row_spec = lambda width: pl.BlockSpec((1, tm, width), lambda i, t: (i, t, 0))
    full = lambda a: pl.BlockSpec(a.shape, lambda i, t: (0,) * a.ndim, pipeline_mode=pl.Buffered(1))
    lw = ln_w.reshape(1, d)
    lb = ln_b.reshape(1, d)
    body = functools.partial(_ffn_body, tm=tm)
    return pl.pallas_call(
        body,
        grid=(b, s // tm),
        in_specs=[row_spec(d), row_spec(d), row_spec(PLE_DIM)]
                 + [full(wup), full(convw), full(wdn), full(wpg), full(wpp), full(lw), full(lb)],
        out_specs=[row_spec(d), row_spec(d)],
        out_shape=[jax.ShapeDtypeStruct((b, s, d), F32), jax.ShapeDtypeStruct((b, s, d), BF16)],
        scratch_shapes=[pltpu.VMEM((8, D_FF), F32)],
        compiler_params=pltpu.CompilerParams(dimension_semantics=("parallel", "arbitrary"),
                                             vmem_limit_bytes=V7X_VMEM_LIMIT_BYTES),
        name="conv_ffn",
    )(x32, xb, p, wup, convw, wdn, wpg, wpp, lw, lb)


def _group_weights(w_in_l):
    offs = np.cumsum((0,) + GROUP_COLS)
    w_ret = w_in_l[:, offs[0]:offs[1]]
    w_rwkv = w_in_l[:, offs[1]:offs[2]]
    w_gla = w_in_l[:, offs[2]:offs[3]]
    w_hgrn = w_in_l[:, offs[3]:offs[4]]
    w_gate = w_in_l[:, offs[4]:offs[5]]
    perm = _ret_perm()
    nq = HEADS * RET_DK
    w_ret = jnp.concatenate([w_ret[:, perm], w_ret[:, nq + perm], w_ret[:, 2 * nq:]], axis=1)
    gq = HEADS * GLA_DK
    o_v, o_gl, o_g = 2 * gq, 2 * gq + BRANCH_WIDTH, 2 * gq + BRANCH_WIDTH + GLA_GATE_LORA
    pad = jnp.zeros((w_gla.shape[0], LANES - GLA_GATE_LORA), w_gla.dtype)
    w_gla = jnp.concatenate([w_gla[:, :o_gl], w_gla[:, o_g:], w_gla[:, o_gl:o_g], pad], axis=1)
    cast = lambda a: a.astype(BF16)
    return cast(w_ret), cast(w_rwkv), cast(w_gla), cast(w_hgrn), cast(w_gate)


def kernel(x, p, ln_in_w, ln_in_b, w_in, rwkv_mu, rwkv_w0, rwkv_w2, rwkv_a0, rwkv_a2, rwkv_g2, rwkv_k_k, rwkv_k_a, rwkv_r_k, rwkv_ln_w, rwkv_ln_b, gla_w2, gla_b, gla_norm_w, hgrn_lb_logits, hgrn_norm_w, w_branch, w_mix_out, ln_mix_w, ln_mix_b, w_ffn_up, ffn_conv, w_ffn_down, w_ple_gate, w_ple_proj, ln_ffn_w, ln_ffn_b):
    b, s, d = x.shape
    depth = w_in.shape[0]
    t = b * s
    v_head = np.arange(BRANCH_WIDTH) // HEAD_DV
    avg = jnp.asarray((v_head[:, None] == v_head[None, :]).astype(np.float32) / HEAD_DV).astype(BF16)
    rope = _rope_tables(s)

    x32, xb = _ln_call(x.reshape(t, d), ln_in_w, ln_in_b)
    for i in range(depth):
        w_ret, w_rwkv, w_gla, w_hgrn, w_gate = _group_weights(w_in[i])
        xb3 = xb.reshape(b, s, d)
        o_ret = _ret_call(xb3, w_ret, rope, avg)
        o_rwkv = _rwkv_call(xb3, w_rwkv, rwkv_mu[i], rwkv_w0[i], rwkv_w2[i], rwkv_a0[i], rwkv_a2[i],
                            rwkv_g2[i], rwkv_k_k[i], rwkv_k_a[i], rwkv_r_k[i], rwkv_ln_w[i], rwkv_ln_b[i], avg)
        w2p = jnp.concatenate([gla_w2[i], jnp.zeros((LANES - GLA_GATE_LORA, gla_w2.shape[2]), F32)], axis=0)
        o_gla = _gla_call(xb3, w_gla, w2p, gla_b[i].reshape(1, -1), jnp.tile(gla_norm_w[i], HEADS).reshape(1, -1), avg)
        o_hgrn = _hgrn_call(xb3, w_hgrn, hgrn_lb_logits.astype(F32), jnp.tile(hgrn_norm_w[i], HEADS).reshape(1, -1),
                            avg, i)
        branches = [o.reshape(t, BRANCH_WIDTH) for o in (o_ret, o_rwkv, o_gla, o_hgrn)]
        x32, xb = _merge_call(x32, xb, branches, w_gate, w_branch[i].astype(BF16), w_mix_out[i].astype(BF16),
                              ln_mix_w[i], ln_mix_b[i])
        x32, xb = _ffn_call(x32.reshape(b, s, d), xb.reshape(b, s, d), p[i], w_ffn_up[i].astype(BF16), ffn_conv[i],
                            w_ffn_down[i].astype(BF16), w_ple_gate[i].astype(BF16), w_ple_proj[i].astype(BF16),
                            ln_ffn_w[i], ln_ffn_b[i])
        x32 = x32.reshape(t, d)
        xb = xb.reshape(t, d)
    return x32.reshape(b, s, d)
```

```python
import functools
import math

import numpy as np
import jax
import jax.numpy as jnp
from jax import lax
from jax.experimental import pallas as pl
from jax.experimental.pallas import tpu as pltpu

F32 = jnp.float32
BF16 = jnp.bfloat16

D_MODEL = 1024
DEPTH = 2
N_BRANCH = 4
HEADS = 4
BRANCH_WIDTH = D_MODEL // 4
HEAD_DV = BRANCH_WIDTH // HEADS
RET_DK = HEAD_DV // 2
GLA_DK = HEAD_DV // 2
HGRN_DK = HEAD_DV
RWKV_N = HEAD_DV
RWKV_W_LORA = 64
RWKV_A_LORA = 64
RWKV_G_LORA = 128
GLA_GATE_LORA = 16
GLA_GATE_TEMP = 16.0
RET_ROPE_BASE = 10000.0
RWKV_DECAY_SCALE = 0.6065306597126334
D_FF = 2816
PLE_DIM = 256
LN_EPS = 1e-5
HEAD_EPS = 1e-6
RWKV_GN_EPS = 64e-5
ALPHA = (2.0 * DEPTH) ** 0.25

RET_COLS = (HEADS * RET_DK, HEADS * RET_DK, BRANCH_WIDTH, BRANCH_WIDTH)
RWKV_COLS = (BRANCH_WIDTH, BRANCH_WIDTH, BRANCH_WIDTH, RWKV_W_LORA, RWKV_A_LORA, RWKV_G_LORA)
GLA_COLS = (HEADS * GLA_DK, HEADS * GLA_DK, BRANCH_WIDTH, GLA_GATE_LORA, BRANCH_WIDTH)
HGRN_COLS = (HEADS * HGRN_DK, HEADS * HGRN_DK, BRANCH_WIDTH, BRANCH_WIDTH)
GROUP_COLS = (sum(RET_COLS), sum(RWKV_COLS), sum(GLA_COLS), sum(HGRN_COLS), N_BRANCH * D_MODEL)

V7X_VMEM_LIMIT_BYTES = 56 * 1024 * 1024
LANES = 128

TIME_TILE = 256
SEQ_PER_STEP = 2
RWKV_SEQ_PER_STEP = 4
RET_CHUNK = 128
GATED_CHUNK = 64
GATED_SUB = 16
RWKV_CHUNK = 64
ROW_TILE = 512


_NN = (((1,), (0,)), ((), ()))
_NT = (((1,), (1,)), ((), ()))
_TN = (((0,), (0,)), ((), ()))


def _split(x, n):
    if x.dtype == BF16:
        return [x]
    parts = []
    r = x
    for i in range(n):
        p = r.astype(BF16)
        parts.append(p)
        if i + 1 < n:
            r = r - p.astype(F32)
    return parts


def _mm(a, b, dims=_NN, na=1, nb=1):
    ap = _split(a, na)
    bp = _split(b, nb)
    order = max(len(ap), len(bp))
    acc = None
    for i, x in enumerate(ap):
        for j, y in enumerate(bp):
            if i + j >= order:
                continue
            t = lax.dot_general(x, y, dims, preferred_element_type=F32)
            acc = t if acc is None else acc + t
    return acc


def _sigmoid(x):
    return 1.0 / (1.0 + jnp.exp(-x))


def _silu(x):
    return x * _sigmoid(x)


def _log_sigmoid(x):
    return jnp.minimum(x, 0.0) - jnp.log1p(jnp.exp(-jnp.abs(x)))


def _gelu_tanh(x):
    return 0.5 * x * (1.0 + jnp.tanh(math.sqrt(2.0 / math.pi) * (x + 0.044715 * (x * x * x))))


def _layer_norm_rows(h, w, b):
    mu = jnp.mean(h, axis=-1, keepdims=True)
    hc = h - mu
    var = jnp.mean(hc * hc, axis=-1, keepdims=True)
    return hc * lax.rsqrt(var + LN_EPS) * w + b


def _head_stack(x, head_of_lane):
    return jnp.concatenate([jnp.where(head_of_lane == h, x, 0.0) for h in range(HEADS)], axis=0)


def _head_unstack(y, head_of_lane, c):
    acc = None
    for h in range(HEADS):
        t = jnp.where(head_of_lane == h, y[h * c:(h + 1) * c], 0.0)
        acc = t if acc is None else acc + t
    return acc


def _seg_mean(x, avg):
    return _mm(x, avg, na=2, nb=1)


def _round_robin(stage_generators):
    live = list(stage_generators)
    while live:
        live = [g for g in live if next(g, _DONE) is not _DONE]


_DONE = object()


def _lane_heads(width, per_head):
    return lax.broadcasted_iota(jnp.int32, (1, width), 1) // per_head


def _ln_body(x_ref, w_ref, b_ref, o32_ref, o16_ref):
    y = _layer_norm_rows(x_ref[...], w_ref[...], b_ref[...])
    o32_ref[...] = y
    o16_ref[...] = y.astype(BF16)


def _ln_call(x2, w, b):
    t, d = x2.shape
    return pl.pallas_call(
        _ln_body,
        grid=(t // ROW_TILE,),
        in_specs=[
            pl.BlockSpec((ROW_TILE, d), lambda i: (i, 0)),
            pl.BlockSpec((1, d), lambda i: (0, 0)),
            pl.BlockSpec((1, d), lambda i: (0, 0)),
        ],
        out_specs=[
            pl.BlockSpec((ROW_TILE, d), lambda i: (i, 0)),
            pl.BlockSpec((ROW_TILE, d), lambda i: (i, 0)),
        ],
        out_shape=[jax.ShapeDtypeStruct((t, d), F32), jax.ShapeDtypeStruct((t, d), BF16)],
        compiler_params=pltpu.CompilerParams(dimension_semantics=("parallel",)),
        name="ln_in",
    )(x2, w.reshape(1, d), b.reshape(1, d))


def _ret_body(x_ref, w_ref, cos_ref, sin_ref, dm_ref, xi_ref, zeta_ref, gc_ref, bd_ref, avg_ref,
              o_ref, z_scr, st_ref, *, bb, ts, c):
    @pl.when(pl.program_id(1) == 0)
    def _():
        st_ref[...] = jnp.zeros_like(st_ref)

    z_scr[...] = jnp.dot(x_ref[...].reshape(bb * ts, x_ref.shape[2]), w_ref[...], preferred_element_type=F32)
    hq = (lax.broadcasted_iota(jnp.int32, (1, 128), 1) % 64) // (RET_DK // 2)
    hv = _lane_heads(BRANCH_WIDTH, HEAD_DV)

    def one(ci, bi):
        trows = pl.ds(pl.multiple_of(ci * c, c), c)
        rows = pl.ds(pl.multiple_of(ci * c + bi * ts, c), c)
        zq = z_scr[rows, 0:128]
        zk = z_scr[rows, 128:256]
        v = z_scr[rows, 256:512]
        g = z_scr[rows, 512:768]
        cs = cos_ref[trows, :]
        sn = sin_ref[trows, :]
        q = zq * cs + pltpu.roll(zq, 64, 1) * sn
        k = (zk * cs + pltpu.roll(zk, 64, 1) * sn) * (RET_DK ** -0.5)
        yield
        sc = _mm(_head_stack(q, hq), k, _NT)
        r_state = st_ref[bi]
        cross = _mm(q * xi_ref[...], r_state)
        st_ref[bi] = r_state * gc_ref[...] + _mm(k, v * zeta_ref[...], _TN) * bd_ref[...]
        p = sc * dm_ref[...]
        yield
        inner = _head_unstack(_mm(p, v), hv, c)
        o = inner + cross
        yield
        mean = _seg_mean(o, avg_ref[...])
        d = o - mean
        yield
        var = _seg_mean(d * d, avg_ref[...])
        y = d * lax.rsqrt(var + HEAD_EPS)
        o_ref[bi, trows, :] = (_silu(g) * y).astype(o_ref.dtype)

    def chunk(ci, carry):
        _round_robin([one(ci, bi) for bi in range(bb)])
        return carry

    lax.fori_loop(0, ts // c, chunk, 0)


def _const_spec(shape):
    nd = len(shape)
    return pl.BlockSpec(shape, lambda b, t: (0,) * nd)


def _ret_tables(c):
    hs = np.arange(HEADS, dtype=np.float32)
    log_gamma = jnp.log1p(-jnp.exp2(-5.0 - jnp.asarray(hs)))
    pos = jnp.arange(c, dtype=F32)
    causal = pos[:, None] >= pos[None, :]
    rel = jnp.where(causal, pos[:, None] - pos[None, :], 0.0)
    decay = jnp.where(causal[None], jnp.exp(log_gamma[:, None, None] * rel[None]), 0.0)
    dm = decay.reshape(HEADS * c, c)
    q_head = (np.arange(128) % 64) // (RET_DK // 2)
    v_head = np.arange(BRANCH_WIDTH) // HEAD_DV
    xi = jnp.exp(log_gamma[None, :] * (pos[:, None] + 1.0))[:, q_head]
    zeta = jnp.exp(log_gamma[None, :] * (c - 1.0 - pos[:, None]))[:, v_head]
    gc = jnp.exp(log_gamma * c)[v_head][None, :]
    bd = jnp.asarray((q_head[:, None] == v_head[None, :]).astype(np.float32))
    return dm, xi, zeta, gc, bd


def _rope_tables(s):
    half = RET_DK // 2
    theta = 1.0 / (RET_ROPE_BASE ** jnp.linspace(0.0, 1.0, half))
    ang = jnp.arange(s, dtype=F32)[:, None] * theta[None, :]
    idx = np.arange(128) % half
    cos = jnp.cos(ang)[:, idx]
    sin = jnp.sin(ang)[:, idx]
    sign = jnp.asarray(np.where(np.arange(128) < 64, -1.0, 1.0).astype(np.float32))
    return cos, sin * sign[None, :]


def _ret_perm():
    half = RET_DK // 2
    l = np.arange(128)
    h = (l % 64) // half
    i = l % half
    return h * RET_DK + 2 * i + (l >= 64)


def _mixer_pallas(body, name, xb, w, args, arg_specs, state_shape, bb, extra_scratch=()):
    b, s, d = xb.shape
    ts = TIME_TILE
    return pl.pallas_call(
        functools.partial(body, bb=bb, ts=ts),
        grid=(b // bb, s // ts),
        in_specs=[pl.BlockSpec((bb, ts, d), lambda i, t: (i, t, 0)), _const_spec(w.shape)] + list(arg_specs),
        out_specs=pl.BlockSpec((bb, ts, BRANCH_WIDTH), lambda i, t: (i, t, 0)),
        out_shape=jax.ShapeDtypeStruct((b, s, BRANCH_WIDTH), BF16),
        scratch_shapes=[pltpu.VMEM((bb * ts, w.shape[1]), F32), pltpu.VMEM((bb,) + tuple(state_shape), F32)]
                       + list(extra_scratch),
        compiler_params=pltpu.CompilerParams(dimension_semantics=("parallel", "arbitrary"),
                                             vmem_limit_bytes=V7X_VMEM_LIMIT_BYTES),
        name=name,
    )(xb, w, *args)


def _ret_call(xb, w, s_tables, avg):
    ts, c = TIME_TILE, RET_CHUNK
    cos, sin = s_tables
    consts = _ret_tables(c) + (avg,)
    body = functools.partial(_ret_body, c=c)
    time_spec = pl.BlockSpec((ts, cos.shape[1]), lambda i, t: (t, 0))
    return _mixer_pallas(body, "retention", xb, w, (cos, sin) + consts,
                         [time_spec, time_spec] + [_const_spec(a.shape) for a in consts],
                         (HEADS * RET_DK, BRANCH_WIDTH), SEQ_PER_STEP)


_MASKED_EXPONENT = -1e30


def _gated_chunk(q, k, v, glog, st_ref, tri, hexp, bdt, hk, hv, c, sub):
    w = q.shape[1]
    nb = c // sub
    vb = v.astype(BF16)
    bcum = _mm(tri, glog, na=1, nb=3)
    yield
    st = st_ref[...]
    cross = _mm(q * jnp.exp(bcum), st, _NT)
    blast = bcum[c - 1:c, :]
    kd = k * jnp.exp(blast - bcum)
    st_ref[...] = st * jnp.exp(blast) + _mm(v, kd, _TN) * bdt
    row = lax.broadcasted_iota(jnp.int32, (c, 1), 0)
    trow = lax.broadcasted_iota(jnp.int32, (sub, 1), 0)
    scores = []
    for i in range(1, nb):
        lo, hi = i * sub, (i + 1) * sub
        ref = bcum[lo - 1:lo, :]
        qt = q[lo:hi] * jnp.exp(bcum[lo:hi] - ref)
        kt = k * jnp.exp(jnp.where(row < lo, ref - bcum, _MASKED_EXPONENT))
        scores.append(_mm(_head_stack(qt, hk), kt, _NT))
    yield
    outs = [cross[0:sub]] + [cross[i * sub:(i + 1) * sub] + _head_unstack(_mm(scores[i - 1], vb), hv, sub)
                             for i in range(1, nb)]
    reps = []
    for i in range(nb):
        lo, hi = i * sub, (i + 1) * sub
        bi, qi, ki = bcum[lo:hi], q[lo:hi], k[lo:hi]
        ps = []
        for s in range(sub):
            bs = jnp.broadcast_to(bi[s:s + 1, :], (sub, w))
            ks = jnp.broadcast_to(ki[s:s + 1, :], (sub, w))
            e = jnp.exp(jnp.where(trow >= s, bi - bs, _MASKED_EXPONENT))
            ps.append(qi * ks * e)
        reps.append(_mm(jnp.concatenate(ps, axis=0), hexp))
    yield
    for i in range(nb):
        vi = v[i * sub:(i + 1) * sub]
        acc = outs[i]
        for s in range(sub):
            acc = acc + reps[i][s * sub:(s + 1) * sub] * jnp.broadcast_to(vi[s:s + 1, :], (sub, BRANCH_WIDTH))
        outs[i] = acc
    return jnp.concatenate(outs, axis=0)


def _gla_body(x_ref, w_ref, w2_ref, b_ref, nw_ref, tri_ref, hexp_ref, bdt_ref, avg_ref,
              o_ref, z_scr, st_ref, *, bb, ts, c, sub):
    @pl.when(pl.program_id(1) == 0)
    def _():
        st_ref[...] = jnp.zeros_like(st_ref)

    z_scr[...] = jnp.dot(x_ref[...].reshape(bb * ts, x_ref.shape[2]), w_ref[...], preferred_element_type=F32)
    hk = _lane_heads(HEADS * GLA_DK, GLA_DK)
    hv = _lane_heads(BRANCH_WIDTH, HEAD_DV)

    def one(ci, bi):
        trows = pl.ds(pl.multiple_of(ci * c, c), c)
        rows = pl.ds(pl.multiple_of(ci * c + bi * ts, c), c)
        q = z_scr[rows, 0:128] * (GLA_DK ** -0.5)
        k = z_scr[rows, 128:256]
        v = z_scr[rows, 256:512]
        g = z_scr[rows, 512:768]
        gl = z_scr[rows, 768:896]
        glog = _log_sigmoid(_mm(gl, w2_ref[...], na=2, nb=2) + b_ref[...]) / GLA_GATE_TEMP
        yield
        o = yield from _gated_chunk(q, k, v, glog, st_ref.at[bi], tri_ref[...], hexp_ref[...], bdt_ref[...],
                                    hk, hv, c, sub)
        yield
        ms = _seg_mean(o * o, avg_ref[...])
        y = o * lax.rsqrt(ms + HEAD_EPS) * nw_ref[...]
        o_ref[bi, trows, :] = (y * _silu(g)).astype(o_ref.dtype)

    def chunk(ci, carry):
        _round_robin([one(ci, bi) for bi in range(bb)])
        return carry

    lax.fori_loop(0, ts // c, chunk, 0)


def _hgrn_body(x_ref, w_ref, lbl_ref, nw_ref, tri_ref, hexp_ref, bdt_ref, avg_ref,
               o_ref, z_scr, st_ref, *, bb, ts, c, sub, layer):
    @pl.when(pl.program_id(1) == 0)
    def _():
        st_ref[...] = jnp.zeros_like(st_ref)

    z_scr[...] = jnp.dot(x_ref[...].reshape(bb * ts, x_ref.shape[2]), w_ref[...], preferred_element_type=F32)
    hk = _lane_heads(HEADS * HGRN_DK, HGRN_DK)
    hv = _lane_heads(BRANCH_WIDTH, HEAD_DV)
    logits = lbl_ref[...]
    ex = jnp.exp(logits - jnp.max(logits, axis=0, keepdims=True))
    prob = ex / jnp.sum(ex, axis=0, keepdims=True)
    lb = jnp.zeros((1, HEADS * HGRN_DK), F32)
    for j in range(1, layer + 1):
        lb = lb + prob[j:j + 1, :]

    def one(ci, bi):
        trows = pl.ds(pl.multiple_of(ci * c, c), c)
        rows = pl.ds(pl.multiple_of(ci * c + bi * ts, c), c)
        q = z_scr[rows, 0:256]
        fz = z_scr[rows, 256:512]
        v = z_scr[rows, 512:768]
        g = z_scr[rows, 768:1024]
        f = lb + (1.0 - lb) * _sigmoid(fz)
        glog = jnp.log(f)
        k = (1.0 - lb) * _sigmoid(-fz)
        o = yield from _gated_chunk(q, k, v, glog, st_ref.at[bi], tri_ref[...], hexp_ref[...], bdt_ref[...],
                                    hk, hv, c, sub)
        yield
        ms = _seg_mean(o * o, avg_ref[...])
        y = o * lax.rsqrt(ms + HEAD_EPS) * nw_ref[...]
        o_ref[bi, trows, :] = (y * _silu(g)).astype(o_ref.dtype)

    def chunk(ci, carry):
        _round_robin([one(ci, bi) for bi in range(bb)])
        return carry

    lax.fori_loop(0, ts // c, chunk, 0)


def _gated_consts(wk, dk, c):
    tri = jnp.asarray(np.tril(np.ones((c, c), np.float32))).astype(BF16)
    k_head = np.arange(wk) // dk
    v_head = np.arange(BRANCH_WIDTH) // HEAD_DV
    same = (k_head[:, None] == v_head[None, :]).astype(np.float32)
    hexp = jnp.asarray(same).astype(BF16)
    bdt = jnp.asarray(same.T)
    return tri, hexp, bdt


def _gla_call(xb, w, w2p, bias, norm_w, avg):
    wk = HEADS * GLA_DK
    args = (w2p, bias, norm_w) + _gated_consts(wk, GLA_DK, GATED_CHUNK) + (avg,)
    body = functools.partial(_gla_body, c=GATED_CHUNK, sub=GATED_SUB)
    return _mixer_pallas(body, "gla", xb, w, args, [_const_spec(a.shape) for a in args], (BRANCH_WIDTH, wk),
                         SEQ_PER_STEP)


def _hgrn_call(xb, w, lb_logits, norm_w, avg, layer):
    wk = HEADS * HGRN_DK
    args = (lb_logits, norm_w) + _gated_consts(wk, HGRN_DK, GATED_CHUNK) + (avg,)
    body = functools.partial(_hgrn_body, c=GATED_CHUNK, sub=GATED_SUB, layer=layer)
    return _mixer_pallas(body, "hgrn2", xb, w, args, [_const_spec(a.shape) for a in args], (BRANCH_WIDTH, wk),
                         SEQ_PER_STEP)


def _rwkv_body(x_ref, w_ref, mu_ref, w0_ref, w2_ref, a0_ref, a2_ref, g2_ref, kk_ref, ka_ref, rk_ref,
               lnw_ref, lnb_ref, tri_ref, ms_ref, mi_ref, eye_ref, avg_ref,
               o_ref, z_scr, st_ref, last_ref, *, bb, ts, c):
    @pl.when(pl.program_id(1) == 0)
    def _():
        st_ref[...] = jnp.zeros_like(st_ref)
        last_ref[...] = jnp.zeros_like(last_ref)

    rowi = lax.broadcasted_iota(jnp.int32, (ts, 1), 0)
    for bi in range(bb):
        z = jnp.dot(x_ref[bi], w_ref[...], preferred_element_type=F32)
        prev = jnp.where(rowi == 0, jnp.broadcast_to(last_ref[bi, 7:8, :], z.shape), pltpu.roll(z, 1, 0))
        last_ref[bi] = z[ts - 8:ts, :]
        z_scr[bi * ts:(bi + 1) * ts, :] = z + (prev - z) * mu_ref[...]

    hl = _lane_heads(BRANCH_WIDTH, RWKV_N)
    mm = _mm

    def one(ci, bi):
        trows = pl.ds(pl.multiple_of(ci * c, c), c)
        rows = pl.ds(pl.multiple_of(ci * c + bi * ts, c), c)
        r = z_scr[rows, 0:256]
        k = z_scr[rows, 256:512]
        v = z_scr[rows, 512:768]
        wl = z_scr[rows, 768:832]
        al = z_scr[rows, 832:896]
        gl = z_scr[rows, 896:1024]
        lw = -RWKV_DECAY_SCALE * _sigmoid(w0_ref[...] + _mm(jnp.tanh(wl), w2_ref[...], na=2, nb=2))
        a = _sigmoid(a0_ref[...] + _mm(al, a2_ref[...], na=2, nb=2))
        g = _mm(_sigmoid(gl), g2_ref[...], na=2, nb=2)
        kk = k * kk_ref[...]
        kk = kk * lax.rsqrt(_mm(kk * kk, avg_ref[...], na=2, nb=1) * float(RWKV_N) + 1e-12)
        k2 = k * (1.0 + (a - 1.0) * ka_ref[...])
        beta = kk * a
        cw = _mm(tri_ref[...], lw, na=1, nb=3)
        e_in = jnp.exp(cw)
        e_out = jnp.exp(-cw)
        at = -kk * jnp.exp(cw - lw)
        rt = r * e_in
        bt = beta * e_out
        kt = k2 * e_out
        n = HEADS * c
        hs_a = _head_stack(at, hl).astype(BF16)
        hs_r = _head_stack(rt, hl)
        lhs_ar = jnp.concatenate([hs_a, hs_r.astype(BF16)], axis=0)
        rhs_bk = jnp.concatenate([bt] * HEADS + [kt] * HEADS, axis=0).astype(BF16)
        yield
        gram = mm(lhs_ar, rhs_bk, _NT)
        a_ab = gram[0:n, 0:n] * ms_ref[...]
        a_ak = (gram[0:n, n:2 * n] * ms_ref[...]).astype(BF16)
        a_rb = (gram[n:2 * n, 0:n] * mi_ref[...]).astype(BF16)
        a_rk = (gram[n:2 * n, n:2 * n] * mi_ref[...]).astype(BF16)
        tinv = eye_ref[...] + a_ab
        pw = a_ab.astype(BF16)
        yield
        pw = mm(pw, pw).astype(BF16)
        v_st = _head_stack(v, hl).astype(BF16)
        av = mm(a_ak, v_st).astype(BF16)
        for _ in range(int(math.log2(c)) - 2):
            yield
            both = mm(jnp.concatenate([pw, tinv.astype(BF16)], axis=0), pw)
            tinv = tinv + both[n:2 * n]
            pw = both[0:n].astype(BF16)
        yield
        tinv = (tinv + mm(tinv, pw)).astype(BF16)
        yield
        w12 = mm(tinv, jnp.concatenate([av, hs_a], axis=1)).astype(BF16)
        yield
        aw = mm(a_rb, w12)
        cwl = cw[c - 1:c, :]
        dec = jnp.exp(cwl - cw)
        hs_bh = _head_stack(beta * dec, hl).astype(BF16)
        hs_kh = _head_stack(k2 * dec, hl).astype(BF16)
        bw = mm(hs_bh, w12, _TN)
        yield
        st = st_ref[bi].astype(BF16)
        wv = BRANCH_WIDTH
        y_st = mm(hs_r + aw[:, wv:2 * wv], st) + aw[:, 0:wv] + mm(a_rk, v_st)
        y = y_st[0:c] + y_st[c:2 * c] + y_st[2 * c:3 * c] + y_st[3 * c:4 * c]
        m_t = eye_ref[...] * jnp.exp(cwl) + bw[:, wv:2 * wv]
        st_ref[bi] = mm(m_t, st) + bw[:, 0:wv] + mm(hs_kh, v_st, _TN)
        yield
        mean = _seg_mean(y, avg_ref[...])
        d = y - mean
        var = _seg_mean(d * d, avg_ref[...])
        yn = d * lax.rsqrt(var + RWKV_GN_EPS) * lnw_ref[...] + lnb_ref[...]
        bonus = _mm(r * k2 * rk_ref[...], avg_ref[...], na=2, nb=1) * float(RWKV_N) * v
        o_ref[bi, trows, :] = ((yn + bonus) * g).astype(o_ref.dtype)

    def chunk(ci, carry):
        _round_robin([one(ci, bi) for bi in range(bb)])
        return carry

    lax.fori_loop(0, ts // c, chunk, 0)


def _rwkv_call(xb, w, mu, w0, w2, a0, a2, g2, k_k, k_a, r_k, ln_w, ln_b, avg):
    b, s, d = xb.shape
    ts, c = TIME_TILE, RWKV_CHUNK
    n = HEADS * c
    tri = jnp.asarray(np.tril(np.ones((c, c), np.float32))).astype(BF16)
    ridx = np.arange(n)
    same = (ridx[:, None] // c) == (ridx[None, :] // c)
    ms = jnp.asarray((same & ((ridx[None, :] % c) < (ridx[:, None] % c))).astype(np.float32))
    mi = jnp.asarray((same & ((ridx[None, :] % c) <= (ridx[:, None] % c))).astype(np.float32))
    eye = jnp.asarray(np.eye(n, dtype=np.float32))
    row = lambda a: a.reshape(1, -1).astype(F32)
    args = (row(mu), row(w0), w2.astype(F32), row(a0), a2.astype(F32), g2.astype(F32),
            row(k_k), row(k_a), row(r_k), row(ln_w), row(ln_b), tri, ms, mi, eye, avg)
    body = functools.partial(_rwkv_body, c=c)
    return _mixer_pallas(body, "rwkv7", xb, w, args, [_const_spec(a.shape) for a in args], (n, BRANCH_WIDTH),
                         RWKV_SEQ_PER_STEP, extra_scratch=[pltpu.VMEM((RWKV_SEQ_PER_STEP, 8, w.shape[1]), F32)])


def _merge_body(x32_ref, xb_ref, b0_ref, b1_ref, b2_ref, b3_ref, wg_ref, wb_ref, wm_ref, lw_ref, lb_ref,
                o32_ref, o16_ref):
    xb = xb_ref[...]
    acc = None
    for n, br_ref in enumerate((b0_ref, b1_ref, b2_ref, b3_ref)):
        gate = jnp.dot(xb, wg_ref[:, n * D_MODEL:(n + 1) * D_MODEL], preferred_element_type=F32)
        br = jnp.dot(br_ref[...], wb_ref[n], preferred_element_type=F32)
        t = _sigmoid(gate) * br
        acc = t if acc is None else acc + t
    y = jnp.dot(acc.astype(BF16), wm_ref[...], preferred_element_type=F32)
    out = _layer_norm_rows(ALPHA * x32_ref[...] + y, lw_ref[...], lb_ref[...])
    o32_ref[...] = out
    o16_ref[...] = out.astype(BF16)


def _merge_call(x32, xb, branches, wg, wb, wm, ln_w, ln_b):
    t, d = x32.shape
    tm = ROW_TILE
    row_spec = lambda width: pl.BlockSpec((tm, width), lambda i: (i, 0))
    full = lambda a: pl.BlockSpec(a.shape, lambda i: (0,) * a.ndim)
    lw = ln_w.reshape(1, d)
    lb = ln_b.reshape(1, d)
    return pl.pallas_call(
        _merge_body,
        grid=(t // tm,),
        in_specs=[row_spec(d), row_spec(d)] + [row_spec(BRANCH_WIDTH)] * N_BRANCH
                 + [full(wg), full(wb), full(wm), full(lw), full(lb)],
        out_specs=[row_spec(d), row_spec(d)],
        out_shape=[jax.ShapeDtypeStruct((t, d), F32), jax.ShapeDtypeStruct((t, d), BF16)],
        compiler_params=pltpu.CompilerParams(dimension_semantics=("parallel",),
                                             vmem_limit_bytes=V7X_VMEM_LIMIT_BYTES),
        name="merge",
    )(x32, xb, *branches, wg, wb, wm, lw, lb)


FFN_CHUNK = 1408


def _ffn_body(x32_ref, xb_ref, p_ref, wup_ref, cw_ref, wdn_ref, wpg_ref, wpp_ref, lw_ref, lb_ref,
              o32_ref, o16_ref, carry_ref, *, tm):
    @pl.when(pl.program_id(1) == 0)
    def _():
        carry_ref[...] = jnp.zeros_like(carry_ref)

    xb = xb_ref[0]
    rowi = lax.broadcasted_iota(jnp.int32, (tm, 1), 0)
    acc = None
    for ci in range(D_FF // FFN_CHUNK):
        lo, hi = ci * FFN_CHUNK, (ci + 1) * FFN_CHUNK
        u = jnp.dot(xb, wup_ref[:, lo:hi], preferred_element_type=F32)
        v = jnp.dot(xb, wup_ref[:, D_FF + lo:D_FF + hi], preferred_element_type=F32)
        prev = carry_ref[:, lo:hi]
        p1 = jnp.broadcast_to(prev[7:8, :], u.shape)
        p2 = jnp.broadcast_to(prev[6:7, :], u.shape)
        u1 = jnp.where(rowi == 0, p1, pltpu.roll(u, 1, 0))
        u2 = jnp.where(rowi == 0, p2, jnp.where(rowi == 1, p1, pltpu.roll(u, 2, 0)))
        carry_ref[:, lo:hi] = u[tm - 8:tm, :]
        uc = cw_ref[0:1, lo:hi] * u2 + cw_ref[1:2, lo:hi] * u1 + cw_ref[2:3, lo:hi] * u
        h = (_gelu_tanh(uc) * v).astype(BF16)
        t = jnp.dot(h, wdn_ref[lo:hi, :], preferred_element_type=F32)
        acc = t if acc is None else acc + t
    ple = _sigmoid(jnp.dot(xb, wpg_ref[...], preferred_element_type=F32)) * jnp.dot(
        p_ref[0].astype(BF16), wpp_ref[...], preferred_element_type=F32)
    out = _layer_norm_rows(ALPHA * x32_ref[0] + acc + ple, lw_ref[...], lb_ref[...])
    o32_ref[0] = out
    o16_ref[0] = out.astype(BF16)


def _ffn_call(x32, xb, p, wup, convw, wdn, wpg, wpp, ln_w, ln_b):
    b, s, d = x32.shape
    tm = ROW_TILE
    row_spec = lambda width: pl.BlockSpec((1, tm, width), lambda i, t: (i, t, 0))
    full = lambda a: pl.BlockSpec(a.shape, lambda i, t: (0,) * a.ndim, pipeline_mode=pl.Buffered(1))
    lw = ln_w.reshape(1, d)
    lb = ln_b.reshape(1, d)
    body = functools.partial(_ffn_body, tm=tm)
    return pl.pallas_call(
        body,
        grid=(b, s // tm),
        in_specs=[row_spec(d), row_spec(d), row_spec(PLE_DIM)]
                 + [full(wup), full(convw), full(wdn), full(wpg), full(wpp), full(lw), full(lb)],
        out_specs=[row_spec(d), row_spec(d)],
        out_shape=[jax.ShapeDtypeStruct((b, s, d), F32), jax.ShapeDtypeStruct((b, s, d), BF16)],
        scratch_shapes=[pltpu.VMEM((8, D_FF), F32)],
        compiler_params=pltpu.CompilerParams(dimension_semantics=("parallel", "arbitrary"),
                                             vmem_limit_bytes=V7X_VMEM_LIMIT_BYTES),
        name="conv_ffn",
    )(x32, xb, p, wup, convw, wdn, wpg, wpp, lw, lb)


def _group_weights(w_in_l):
    offs = np.cumsum((0,) + GROUP_COLS)
    w_ret = w_in_l[:, offs[0]:offs[1]]
    w_rwkv = w_in_l[:, offs[1]:offs[2]]
    w_gla = w_in_l[:, offs[2]:offs[3]]
    w_hgrn = w_in_l[:, offs[3]:offs[4]]
    w_gate = w_in_l[:, offs[4]:offs[5]]
    perm = _ret_perm()
    nq = HEADS * RET_DK
    w_ret = jnp.concatenate([w_ret[:, perm], w_ret[:, nq + perm], w_ret[:, 2 * nq:]], axis=1)
    gq = HEADS * GLA_DK
    o_v, o_gl, o_g = 2 * gq, 2 * gq + BRANCH_WIDTH, 2 * gq + BRANCH_WIDTH + GLA_GATE_LORA
    pad = jnp.zeros((w_gla.shape[0], LANES - GLA_GATE_LORA), w_gla.dtype)
    w_gla = jnp.concatenate([w_gla[:, :o_gl], w_gla[:, o_g:], w_gla[:, o_gl:o_g], pad], axis=1)
    cast = lambda a: a.astype(BF16)
    return cast(w_ret), cast(w_rwkv), cast(w_gla), cast(w_hgrn), cast(w_gate)


def kernel(x, p, ln_in_w, ln_in_b, w_in, rwkv_mu, rwkv_w0, rwkv_w2, rwkv_a0, rwkv_a2, rwkv_g2, rwkv_k_k, rwkv_k_a, rwkv_r_k, rwkv_ln_w, rwkv_ln_b, gla_w2, gla_b, gla_norm_w, hgrn_lb_logits, hgrn_norm_w, w_branch, w_mix_out, ln_mix_w, ln_mix_b, w_ffn_up, ffn_conv, w_ffn_down, w_ple_gate, w_ple_proj, ln_ffn_w, ln_ffn_b):
    b, s, d = x.shape
    depth = w_in.shape[0]
    t = b * s
    v_head = np.arange(BRANCH_WIDTH) // HEAD_DV
    avg = jnp.asarray((v_head[:, None] == v_head[None, :]).astype(np.float32) / HEAD_DV).astype(BF16)
    rope = _rope_tables(s)

    x32, xb = _ln_call(x.reshape(t, d), ln_in_w, ln_in_b)
    for i in range(depth):
        w_ret, w_rwkv, w_gla, w_hgrn, w_gate = _group_weights(w_in[i])
        xb3 = xb.reshape(b, s, d)
        o_ret = _ret_call(xb3, w_ret, rope, avg)
        o_rwkv = _rwkv_call(xb3, w_rwkv, rwkv_mu[i], rwkv_w0[i], rwkv_w2[i], rwkv_a0[i], rwkv_a2[i],
                            rwkv_g2[i], rwkv_k_k[i], rwkv_k_a[i], rwkv_r_k[i], rwkv_ln_w[i], rwkv_ln_b[i], avg)
        w2p = jnp.concatenate([gla_w2[i], jnp.zeros((LANES - GLA_GATE_LORA, gla_w2.shape[2]), F32)], axis=0)
        o_gla = _gla_call(xb3, w_gla, w2p, gla_b[i].reshape(1, -1), jnp.tile(gla_norm_w[i], HEADS).reshape(1, -1), avg)
        o_hgrn = _hgrn_call(xb3, w_hgrn, hgrn_lb_logits.astype(F32), jnp.tile(hgrn_norm_w[i], HEADS).reshape(1, -1),
                            avg, i)
        branches = [o.reshape(t, BRANCH_WIDTH) for o in (o_ret, o_rwkv, o_gla, o_hgrn)]
        x32, xb = _merge_call(x32, xb, branches, w_gate, w_branch[i].astype(BF16), w_mix_out[i].astype(BF16),
                              ln_mix_w[i], ln_mix_b[i])
        x32, xb = _ffn_call(x32.reshape(b, s, d), xb.reshape(b, s, d), p[i], w_ffn_up[i].astype(BF16), ffn_conv[i],
                            w_ffn_down[i].astype(BF16), w_ple_gate[i].astype(BF16), w_ple_proj[i].astype(BF16),
                            ln_ffn_w[i], ln_ffn_b[i])
        x32 = x32.reshape(t, d)
        xb = xb.reshape(t, d)
    return x32.reshape(b, s, d)
```

```python
import functools
import math

import numpy as np
import jax
import jax.numpy as jnp
from jax import lax
from jax.experimental import pallas as pl
from jax.experimental.pallas import tpu as pltpu

F32 = jnp.float32
BF16 = jnp.bfloat16

D_MODEL = 1024
DEPTH = 2
N_BRANCH = 4
HEADS = 4
BRANCH_WIDTH = D_MODEL // 4
HEAD_DV = BRANCH_WIDTH // HEADS
RET_DK = HEAD_DV // 2
GLA_DK = HEAD_DV // 2
HGRN_DK = HEAD_DV
RWKV_N = HEAD_DV
RWKV_W_LORA = 64
RWKV_A_LORA = 64
RWKV_G_LORA = 128
GLA_GATE_LORA = 16
GLA_GATE_TEMP = 16.0
RET_ROPE_BASE = 10000.0
RWKV_DECAY_SCALE = 0.6065306597126334
D_FF = 2816
PLE_DIM = 256
LN_EPS = 1e-5
HEAD_EPS = 1e-6
RWKV_GN_EPS = 64e-5
ALPHA = (2.0 * DEPTH) ** 0.25

RET_COLS = (HEADS * RET_DK, HEADS * RET_DK, BRANCH_WIDTH, BRANCH_WIDTH)
RWKV_COLS = (BRANCH_WIDTH, BRANCH_WIDTH, BRANCH_WIDTH, RWKV_W_LORA, RWKV_A_LORA, RWKV_G_LORA)
GLA_COLS = (HEADS * GLA_DK, HEADS * GLA_DK, BRANCH_WIDTH, GLA_GATE_LORA, BRANCH_WIDTH)
HGRN_COLS = (HEADS * HGRN_DK, HEADS * HGRN_DK, BRANCH_WIDTH, BRANCH_WIDTH)
GROUP_COLS = (sum(RET_COLS), sum(RWKV_COLS), sum(GLA_COLS), sum(HGRN_COLS), N_BRANCH * D_MODEL)

V7X_VMEM_LIMIT_BYTES = 56 * 1024 * 1024
LANES = 128

TIME_TILE = 256
SEQ_PER_STEP = 4
RWKV_SEQ_PER_STEP = 4
RET_CHUNK = 128
GATED_CHUNK = 64
GATED_SUB = 8
RWKV_CHUNK = 64
ROW_TILE = 512


_NN = (((1,), (0,)), ((), ()))
_NT = (((1,), (1,)), ((), ()))
_TN = (((0,), (0,)), ((), ()))


def _split(x, n):
    if x.dtype == BF16:
        return [x]
    parts = []
    r = x
    for i in range(n):
        p = r.astype(BF16)
        parts.append(p)
        if i + 1 < n:
            r = r - p.astype(F32)
    return parts


def _mm(a, b, dims=_NN, na=1, nb=1):
    ap = _split(a, na)
    bp = _split(b, nb)
    order = max(len(ap), len(bp))
    acc = None
    for i, x in enumerate(ap):
        for j, y in enumerate(bp):
            if i + j >= order:
                continue
            t = lax.dot_general(x, y, dims, preferred_element_type=F32)
            acc = t if acc is None else acc + t
    return acc


def _sigmoid(x):
    return 1.0 / (1.0 + jnp.exp(-x))


def _silu(x):
    return x * _sigmoid(x)


def _log_sigmoid(x):
    return jnp.minimum(x, 0.0) - jnp.log1p(jnp.exp(-jnp.abs(x)))


def _gelu_tanh(x):
    return 0.5 * x * (1.0 + jnp.tanh(math.sqrt(2.0 / math.pi) * (x + 0.044715 * (x * x * x))))


def _layer_norm_rows(h, w, b):
    mu = jnp.mean(h, axis=-1, keepdims=True)
    hc = h - mu
    var = jnp.mean(hc * hc, axis=-1, keepdims=True)
    return hc * lax.rsqrt(var + LN_EPS) * w + b


def _head_stack(x, head_of_lane):
    return jnp.concatenate([jnp.where(head_of_lane == h, x, 0.0) for h in range(HEADS)], axis=0)


def _head_unstack(y, head_of_lane, c):
    acc = None
    for h in range(HEADS):
        t = jnp.where(head_of_lane == h, y[h * c:(h + 1) * c], 0.0)
        acc = t if acc is None else acc + t
    return acc


def _seg_mean(x, avg):
    return _mm(x, avg, na=2, nb=1)


def _round_robin(stage_generators):
    live = list(stage_generators)
    while live:
        live = [g for g in live if next(g, _DONE) is not _DONE]


_DONE = object()


def _lane_heads(width, per_head):
    return lax.broadcasted_iota(jnp.int32, (1, width), 1) // per_head


def _ln_body(x_ref, w_ref, b_ref, o32_ref, o16_ref):
    y = _layer_norm_rows(x_ref[...], w_ref[...], b_ref[...])
    o32_ref[...] = y
    o16_ref[...] = y.astype(BF16)


def _ln_call(x2, w, b):
    t, d = x2.shape
    return pl.pallas_call(
        _ln_body,
        grid=(t // ROW_TILE,),
        in_specs=[
            pl.BlockSpec((ROW_TILE, d), lambda i: (i, 0)),
            pl.BlockSpec((1, d), lambda i: (0, 0)),
            pl.BlockSpec((1, d), lambda i: (0, 0)),
        ],
        out_specs=[
            pl.BlockSpec((ROW_TILE, d), lambda i: (i, 0)),
            pl.BlockSpec((ROW_TILE, d), lambda i: (i, 0)),
        ],
        out_shape=[jax.ShapeDtypeStruct((t, d), F32), jax.ShapeDtypeStruct((t, d), BF16)],
        compiler_params=pltpu.CompilerParams(dimension_semantics=("parallel",)),
        name="ln_in",
    )(x2, w.reshape(1, d), b.reshape(1, d))


def _ret_body(x_ref, w_ref, cos_ref, sin_ref, dm_ref, xi_ref, zeta_ref, gc_ref, bd_ref, avg_ref,
              o_ref, z_scr, st_ref, *, bb, ts, c):
    @pl.when(pl.program_id(1) == 0)
    def _():
        st_ref[...] = jnp.zeros_like(st_ref)

    z_scr[...] = jnp.dot(x_ref[...].reshape(bb * ts, x_ref.shape[2]), w_ref[...], preferred_element_type=F32)
    hq = (lax.broadcasted_iota(jnp.int32, (1, 128), 1) % 64) // (RET_DK // 2)
    hv = _lane_heads(BRANCH_WIDTH, HEAD_DV)

    def one(ci, bi):
        trows = pl.ds(pl.multiple_of(ci * c, c), c)
        rows = pl.ds(pl.multiple_of(ci * c + bi * ts, c), c)
        zq = z_scr[rows, 0:128]
        zk = z_scr[rows, 128:256]
        v = z_scr[rows, 256:512]
        g = z_scr[rows, 512:768]
        cs = cos_ref[trows, :]
        sn = sin_ref[trows, :]
        q = zq * cs + pltpu.roll(zq, 64, 1) * sn
        k = (zk * cs + pltpu.roll(zk, 64, 1) * sn) * (RET_DK ** -0.5)
        yield
        sc = _mm(_head_stack(q, hq), k, _NT)
        r_state = st_ref[bi]
        cross = _mm(q * xi_ref[...], r_state)
        st_ref[bi] = r_state * gc_ref[...] + _mm(k, v * zeta_ref[...], _TN) * bd_ref[...]
        p = sc * dm_ref[...]
        yield
        inner = _head_unstack(_mm(p, v), hv, c)
        o = inner + cross
        yield
        mean = _seg_mean(o, avg_ref[...])
        d = o - mean
        yield
        var = _seg_mean(d * d, avg_ref[...])
        y = d * lax.rsqrt(var + HEAD_EPS)
        o_ref[bi, trows, :] = (_silu(g) * y).astype(o_ref.dtype)

    def chunk(ci, carry):
        _round_robin([one(ci, bi) for bi in range(bb)])
        return carry

    lax.fori_loop(0, ts // c, chunk, 0)


def _const_spec(shape):
    nd = len(shape)
    return pl.BlockSpec(shape, lambda b, t: (0,) * nd)


def _ret_tables(c):
    hs = np.arange(HEADS, dtype=np.float32)
    log_gamma = jnp.log1p(-jnp.exp2(-5.0 - jnp.asarray(hs)))
    pos = jnp.arange(c, dtype=F32)
    causal = pos[:, None] >= pos[None, :]
    rel = jnp.where(causal, pos[:, None] - pos[None, :], 0.0)
    decay = jnp.where(causal[None], jnp.exp(log_gamma[:, None, None] * rel[None]), 0.0)
    dm = decay.reshape(HEADS * c, c)
    q_head = (np.arange(128) % 64) // (RET_DK // 2)
    v_head = np.arange(BRANCH_WIDTH) // HEAD_DV
    xi = jnp.exp(log_gamma[None, :] * (pos[:, None] + 1.0))[:, q_head]
    zeta = jnp.exp(log_gamma[None, :] * (c - 1.0 - pos[:, None]))[:, v_head]
    gc = jnp.exp(log_gamma * c)[v_head][None, :]
    bd = jnp.asarray((q_head[:, None] == v_head[None, :]).astype(np.float32))
    return dm, xi, zeta, gc, bd


def _rope_tables(s):
    half = RET_DK // 2
    theta = 1.0 / (RET_ROPE_BASE ** jnp.linspace(0.0, 1.0, half))
    ang = jnp.arange(s, dtype=F32)[:, None] * theta[None, :]
    idx = np.arange(128) % half
    cos = jnp.cos(ang)[:, idx]
    sin = jnp.sin(ang)[:, idx]
    sign = jnp.asarray(np.where(np.arange(128) < 64, -1.0, 1.0).astype(np.float32))
    return cos, sin * sign[None, :]


def _ret_perm():
    half = RET_DK // 2
    l = np.arange(128)
    h = (l % 64) // half
    i = l % half
    return h * RET_DK + 2 * i + (l >= 64)


def _mixer_pallas(body, name, xb, w, args, arg_specs, state_shape, bb, extra_scratch=()):
    b, s, d = xb.shape
    ts = TIME_TILE
    return pl.pallas_call(
        functools.partial(body, bb=bb, ts=ts),
        grid=(b // bb, s // ts),
        in_specs=[pl.BlockSpec((bb, ts, d), lambda i, t: (i, t, 0)), _const_spec(w.shape)] + list(arg_specs),
        out_specs=pl.BlockSpec((bb, ts, BRANCH_WIDTH), lambda i, t: (i, t, 0)),
        out_shape=jax.ShapeDtypeStruct((b, s, BRANCH_WIDTH), BF16),
        scratch_shapes=[pltpu.VMEM((bb * ts, w.shape[1]), F32), pltpu.VMEM((bb,) + tuple(state_shape), F32)]
                       + list(extra_scratch),
        compiler_params=pltpu.CompilerParams(dimension_semantics=("parallel", "arbitrary"),
                                             vmem_limit_bytes=V7X_VMEM_LIMIT_BYTES),
        name=name,
    )(xb, w, *args)


def _ret_call(xb, w, s_tables, avg):
    ts, c = TIME_TILE, RET_CHUNK
    cos, sin = s_tables
    consts = _ret_tables(c) + (avg,)
    body = functools.partial(_ret_body, c=c)
    time_spec = pl.BlockSpec((ts, cos.shape[1]), lambda i, t: (t, 0))
    return _mixer_pallas(body, "retention", xb, w, (cos, sin) + consts,
                         [time_spec, time_spec] + [_const_spec(a.shape) for a in consts],
                         (HEADS * RET_DK, BRANCH_WIDTH), SEQ_PER_STEP)


_MASKED_EXPONENT = -1e30
_LOG2E = math.log2(math.e)


def _gated_chunk(q, k, v, glog, st_ref, tri, hexp, bdt, hk, hv, c, sub):
    w = q.shape[1]
    nb = c // sub
    vb = v.astype(BF16)
    bcum = _mm(tri, glog, na=1, nb=3) * _LOG2E
    yield
    st = st_ref[...]
    cross = _mm(q * jnp.exp2(bcum), st, _NT)
    blast = bcum[c - 1:c, :]
    kd = k * jnp.exp2(blast - bcum)
    st_ref[...] = st * jnp.exp2(blast) + _mm(v, kd, _TN) * bdt
    row = lax.broadcasted_iota(jnp.int32, (c, 1), 0)
    trow = lax.broadcasted_iota(jnp.int32, (sub, 1), 0)
    scores = []
    for i in range(1, nb):
        lo, hi = i * sub, (i + 1) * sub
        ref = bcum[lo - 1:lo, :]
        qt = q[lo:hi] * jnp.exp2(bcum[lo:hi] - ref)
        kt = k * jnp.exp2(jnp.where(row < lo, ref - bcum, _MASKED_EXPONENT))
        scores.append(_mm(_head_stack(qt, hk), kt, _NT))
    yield
    outs = [cross[0:sub]] + [cross[i * sub:(i + 1) * sub] + _head_unstack(_mm(scores[i - 1], vb), hv, sub)
                             for i in range(1, nb)]
    ps = []
    for i in range(nb):
        lo, hi = i * sub, (i + 1) * sub
        bi, qi, ki = bcum[lo:hi], q[lo:hi], k[lo:hi]
        for s in range(sub):
            bs = jnp.broadcast_to(bi[s:s + 1, :], (sub, w))
            ks = jnp.broadcast_to(ki[s:s + 1, :], (sub, w))
            e = jnp.exp2(jnp.where(trow >= s, bi - bs, _MASKED_EXPONENT))
            ps.append(qi * ks * e)
    rep = _mm(jnp.concatenate(ps, axis=0), hexp)
    yield
    for i in range(nb):
        vi = v[i * sub:(i + 1) * sub]
        acc = outs[i]
        for s in range(sub):
            r0 = (i * sub + s) * sub
            acc = acc + rep[r0:r0 + sub] * jnp.broadcast_to(vi[s:s + 1, :], (sub, BRANCH_WIDTH))
        outs[i] = acc
    return jnp.concatenate(outs, axis=0)


def _gla_body(x_ref, w_ref, w2_ref, b_ref, nw_ref, tri_ref, hexp_ref, bdt_ref, avg_ref,
              o_ref, z_scr, st_ref, *, bb, ts, c, sub):
    @pl.when(pl.program_id(1) == 0)
    def _():
        st_ref[...] = jnp.zeros_like(st_ref)

    z_scr[...] = jnp.dot(x_ref[...].reshape(bb * ts, x_ref.shape[2]), w_ref[...], preferred_element_type=F32)
    hk = _lane_heads(HEADS * GLA_DK, GLA_DK)
    hv = _lane_heads(BRANCH_WIDTH, HEAD_DV)

    def one(ci, bi):
        trows = pl.ds(pl.multiple_of(ci * c, c), c)
        rows = pl.ds(pl.multiple_of(ci * c + bi * ts, c), c)
        q = z_scr[rows, 0:128] * (GLA_DK ** -0.5)
        k = z_scr[rows, 128:256]
        v = z_scr[rows, 256:512]
        g = z_scr[rows, 512:768]
        gl = z_scr[rows, 768:896]
        glog = _log_sigmoid(_mm(gl, w2_ref[...], na=2, nb=2) + b_ref[...]) / GLA_GATE_TEMP
        yield
        o = yield from _gated_chunk(q, k, v, glog, st_ref.at[bi], tri_ref[...], hexp_ref[...], bdt_ref[...],
                                    hk, hv, c, sub)
        yield
        ms = _seg_mean(o * o, avg_ref[...])
        y = o * lax.rsqrt(ms + HEAD_EPS) * nw_ref[...]
        o_ref[bi, trows, :] = (y * _silu(g)).astype(o_ref.dtype)

    def chunk(ci, carry):
        _round_robin([one(ci, bi) for bi in range(bb)])
        return carry

    lax.fori_loop(0, ts // c, chunk, 0)


def _hgrn_body(x_ref, w_ref, lbl_ref, nw_ref, tri_ref, hexp_ref, bdt_ref, avg_ref,
               o_ref, z_scr, st_ref, *, bb, ts, c, sub, layer):
    @pl.when(pl.program_id(1) == 0)
    def _():
        st_ref[...] = jnp.zeros_like(st_ref)

    z_scr[...] = jnp.dot(x_ref[...].reshape(bb * ts, x_ref.shape[2]), w_ref[...], preferred_element_type=F32)
    hk = _lane_heads(HEADS * HGRN_DK, HGRN_DK)
    hv = _lane_heads(BRANCH_WIDTH, HEAD_DV)
    logits = lbl_ref[...]
    ex = jnp.exp(logits - jnp.max(logits, axis=0, keepdims=True))
    prob = ex / jnp.sum(ex, axis=0, keepdims=True)
    lb = jnp.zeros((1, HEADS * HGRN_DK), F32)
    for j in range(1, layer + 1):
        lb = lb + prob[j:j + 1, :]

    def one(ci, bi):
        trows = pl.ds(pl.multiple_of(ci * c, c), c)
        rows = pl.ds(pl.multiple_of(ci * c + bi * ts, c), c)
        q = z_scr[rows, 0:256]
        fz = z_scr[rows, 256:512]
        v = z_scr[rows, 512:768]
        g = z_scr[rows, 768:1024]
        f = lb + (1.0 - lb) * _sigmoid(fz)
        glog = jnp.log(f)
        k = (1.0 - lb) * _sigmoid(-fz)
        o = yield from _gated_chunk(q, k, v, glog, st_ref.at[bi], tri_ref[...], hexp_ref[...], bdt_ref[...],
                                    hk, hv, c, sub)
        yield
        ms = _seg_mean(o * o, avg_ref[...])
        y = o * lax.rsqrt(ms + HEAD_EPS) * nw_ref[...]
        o_ref[bi, trows, :] = (y * _silu(g)).astype(o_ref.dtype)

    def chunk(ci, carry):
        _round_robin([one(ci, bi) for bi in range(bb)])
        return carry

    lax.fori_loop(0, ts // c, chunk, 0)


def _gated_consts(wk, dk, c):
    tri = jnp.asarray(np.tril(np.ones((c, c), np.float32))).astype(BF16)
    k_head = np.arange(wk) // dk
    v_head = np.arange(BRANCH_WIDTH) // HEAD_DV
    same = (k_head[:, None] == v_head[None, :]).astype(np.float32)
    hexp = jnp.asarray(same).astype(BF16)
    bdt = jnp.asarray(same.T)
    return tri, hexp, bdt


def _gla_call(xb, w, w2p, bias, norm_w, avg):
    wk = HEADS * GLA_DK
    args = (w2p, bias, norm_w) + _gated_consts(wk, GLA_DK, GATED_CHUNK) + (avg,)
    body = functools.partial(_gla_body, c=GATED_CHUNK, sub=GATED_SUB)
    return _mixer_pallas(body, "gla", xb, w, args, [_const_spec(a.shape) for a in args], (BRANCH_WIDTH, wk),
                         SEQ_PER_STEP)


def _hgrn_call(xb, w, lb_logits, norm_w, avg, layer):
    wk = HEADS * HGRN_DK
    args = (lb_logits, norm_w) + _gated_consts(wk, HGRN_DK, GATED_CHUNK) + (avg,)
    body = functools.partial(_hgrn_body, c=GATED_CHUNK, sub=GATED_SUB, layer=layer)
    return _mixer_pallas(body, "hgrn2", xb, w, args, [_const_spec(a.shape) for a in args], (BRANCH_WIDTH, wk),
                         SEQ_PER_STEP)


def _rwkv_body(x_ref, w_ref, mu_ref, w0_ref, w2_ref, a0_ref, a2_ref, g2_ref, kk_ref, ka_ref, rk_ref,
               lnw_ref, lnb_ref, tri_ref, ms_ref, mi_ref, eye_ref, avg_ref,
               o_ref, z_scr, st_ref, last_ref, *, bb, ts, c):
    @pl.when(pl.program_id(1) == 0)
    def _():
        st_ref[...] = jnp.zeros_like(st_ref)
        last_ref[...] = jnp.zeros_like(last_ref)

    rowi = lax.broadcasted_iota(jnp.int32, (ts, 1), 0)
    for bi in range(bb):
        z = jnp.dot(x_ref[bi], w_ref[...], preferred_element_type=F32)
        prev = jnp.where(rowi == 0, jnp.broadcast_to(last_ref[bi, 7:8, :], z.shape), pltpu.roll(z, 1, 0))
        last_ref[bi] = z[ts - 8:ts, :]
        z_scr[bi * ts:(bi + 1) * ts, :] = z + (prev - z) * mu_ref[...]

    hl = _lane_heads(BRANCH_WIDTH, RWKV_N)
    mm = _mm

    def one(ci, bi):
        trows = pl.ds(pl.multiple_of(ci * c, c), c)
        rows = pl.ds(pl.multiple_of(ci * c + bi * ts, c), c)
        r = z_scr[rows, 0:256]
        k = z_scr[rows, 256:512]
        v = z_scr[rows, 512:768]
        wl = z_scr[rows, 768:832]
        al = z_scr[rows, 832:896]
        gl = z_scr[rows, 896:1024]
        lw = -RWKV_DECAY_SCALE * _sigmoid(w0_ref[...] + _mm(jnp.tanh(wl), w2_ref[...], na=2, nb=2))
        a = _sigmoid(a0_ref[...] + _mm(al, a2_ref[...], na=2, nb=2))
        g = _mm(_sigmoid(gl), g2_ref[...], na=2, nb=2)
        kk = k * kk_ref[...]
        kk = kk * lax.rsqrt(_mm(kk * kk, avg_ref[...], na=2, nb=1) * float(RWKV_N) + 1e-12)
        k2 = k * (1.0 + (a - 1.0) * ka_ref[...])
        beta = kk * a
        cw = _mm(tri_ref[...], lw, na=1, nb=3)
        e_in = jnp.exp(cw)
        e_out = jnp.exp(-cw)
        at = -kk * jnp.exp(cw - lw)
        rt = r * e_in
        bt = beta * e_out
        kt = k2 * e_out
        n = HEADS * c
        hs_a = _head_stack(at, hl).astype(BF16)
        hs_r = _head_stack(rt, hl)
        lhs_ar = jnp.concatenate([hs_a, hs_r.astype(BF16)], axis=0)
        rhs_bk = jnp.concatenate([bt] * HEADS + [kt] * HEADS, axis=0).astype(BF16)
        yield
        gram = mm(lhs_ar, rhs_bk, _NT)
        a_ab = gram[0:n, 0:n] * ms_ref[...]
        a_ak = (gram[0:n, n:2 * n] * ms_ref[...]).astype(BF16)
        a_rb = (gram[n:2 * n, 0:n] * mi_ref[...]).astype(BF16)
        a_rk = (gram[n:2 * n, n:2 * n] * mi_ref[...]).astype(BF16)
        tinv = eye_ref[...] + a_ab
        pw = a_ab.astype(BF16)
        yield
        pw = mm(pw, pw).astype(BF16)
        v_st = _head_stack(v, hl).astype(BF16)
        av = mm(a_ak, v_st).astype(BF16)
        for _ in range(int(math.log2(c)) - 2):
            yield
            both = mm(jnp.concatenate([pw, tinv.astype(BF16)], axis=0), pw)
            tinv = tinv + both[n:2 * n]
            pw = both[0:n].astype(BF16)
        yield
        tinv = (tinv + mm(tinv, pw)).astype(BF16)
        yield
        w12 = mm(tinv, jnp.concatenate([av, hs_a], axis=1)).astype(BF16)
        yield
        aw = mm(a_rb, w12)
        cwl = cw[c - 1:c, :]
        dec = jnp.exp(cwl - cw)
        hs_bh = _head_stack(beta * dec, hl).astype(BF16)
        hs_kh = _head_stack(k2 * dec, hl).astype(BF16)
        bw = mm(hs_bh, w12, _TN)
        yield
        st = st_ref[bi].astype(BF16)
        wv = BRANCH_WIDTH
        y_st = mm(hs_r + aw[:, wv:2 * wv], st) + aw[:, 0:wv] + mm(a_rk, v_st)
        y = y_st[0:c] + y_st[c:2 * c] + y_st[2 * c:3 * c] + y_st[3 * c:4 * c]
        m_t = eye_ref[...] * jnp.exp(cwl) + bw[:, wv:2 * wv]
        st_ref[bi] = mm(m_t, st) + bw[:, 0:wv] + mm(hs_kh, v_st, _TN)
        yield
        mean = _seg_mean(y, avg_ref[...])
        d = y - mean
        var = _seg_mean(d * d, avg_ref[...])
        yn = d * lax.rsqrt(var + RWKV_GN_EPS) * lnw_ref[...] + lnb_ref[...]
        bonus = _mm(r * k2 * rk_ref[...], avg_ref[...], na=2, nb=1) * float(RWKV_N) * v
        o_ref[bi, trows, :] = ((yn + bonus) * g).astype(o_ref.dtype)

    def chunk(ci, carry):
        _round_robin([one(ci, bi) for bi in range(bb)])
        return carry

    lax.fori_loop(0, ts // c, chunk, 0)


def _rwkv_call(xb, w, mu, w0, w2, a0, a2, g2, k_k, k_a, r_k, ln_w, ln_b, avg):
    b, s, d = xb.shape
    ts, c = TIME_TILE, RWKV_CHUNK
    n = HEADS * c
    tri = jnp.asarray(np.tril(np.ones((c, c), np.float32))).astype(BF16)
    ridx = np.arange(n)
    same = (ridx[:, None] // c) == (ridx[None, :] // c)
    ms = jnp.asarray((same & ((ridx[None, :] % c) < (ridx[:, None] % c))).astype(np.float32))
    mi = jnp.asarray((same & ((ridx[None, :] % c) <= (ridx[:, None] % c))).astype(np.float32))
    eye = jnp.asarray(np.eye(n, dtype=np.float32))
    row = lambda a: a.reshape(1, -1).astype(F32)
    args = (row(mu), row(w0), w2.astype(F32), row(a0), a2.astype(F32), g2.astype(F32),
            row(k_k), row(k_a), row(r_k), row(ln_w), row(ln_b), tri, ms, mi, eye, avg)
    body = functools.partial(_rwkv_body, c=c)
    return _mixer_pallas(body, "rwkv7", xb, w, args, [_const_spec(a.shape) for a in args], (n, BRANCH_WIDTH),
                         RWKV_SEQ_PER_STEP, extra_scratch=[pltpu.VMEM((RWKV_SEQ_PER_STEP, 8, w.shape[1]), F32)])


def _merge_body(x32_ref, xb_ref, b0_ref, b1_ref, b2_ref, b3_ref, wg_ref, wb_ref, wm_ref, lw_ref, lb_ref,
                o32_ref, o16_ref):
    xb = xb_ref[...]
    acc = None
    for n, br_ref in enumerate((b0_ref, b1_ref, b2_ref, b3_ref)):
        gate = jnp.dot(xb, wg_ref[:, n * D_MODEL:(n + 1) * D_MODEL], preferred_element_type=F32)
        br = jnp.dot(br_ref[...], wb_ref[n], preferred_element_type=F32)
        t = _sigmoid(gate) * br
        acc = t if acc is None else acc + t
    y = jnp.dot(acc.astype(BF16), wm_ref[...], preferred_element_type=F32)
    out = _layer_norm_rows(ALPHA * x32_ref[...] + y, lw_ref[...], lb_ref[...])
    o32_ref[...] = out
    o16_ref[...] = out.astype(BF16)


def _merge_call(x32, xb, branches, wg, wb, wm, ln_w, ln_b):
    t, d = x32.shape
    tm = ROW_TILE
    row_spec = lambda width: pl.BlockSpec((tm, width), lambda i: (i, 0))
    full = lambda a: pl.BlockSpec(a.shape, lambda i: (0,) * a.ndim)
    lw = ln_w.reshape(1, d)
    lb = ln_b.reshape(1, d)
    return pl.pallas_call(
        _merge_body,
        grid=(t // tm,),
        in_specs=[row_spec(d), row_spec(d)] + [row_spec(BRANCH_WIDTH)] * N_BRANCH
                 + [full(wg), full(wb), full(wm), full(lw), full(lb)],
        out_specs=[row_spec(d), row_spec(d)],
        out_shape=[jax.ShapeDtypeStruct((t, d), F32), jax.ShapeDtypeStruct((t, d), BF16)],
        compiler_params=pltpu.CompilerParams(dimension_semantics=("parallel",),
                                             vmem_limit_bytes=V7X_VMEM_LIMIT_BYTES),
        name="merge",
    )(x32, xb, *branches, wg, wb, wm, lw, lb)


FFN_CHUNK = 1408


def _ffn_body(x32_ref, xb_ref, p_ref, wup_ref, cw_ref, wdn_ref, wpg_ref, wpp_ref, lw_ref, lb_ref,
              o32_ref, o16_ref, carry_ref, *, tm):
    @pl.when(pl.program_id(1) == 0)
    def _():
        carry_ref[...] = jnp.zeros_like(carry_ref)

    xb = xb_ref[0]
    rowi = lax.broadcasted_iota(jnp.int32, (tm, 1), 0)
    acc = None
    for ci in range(D_FF // FFN_CHUNK):
        lo, hi = ci * FFN_CHUNK, (ci + 1) * FFN_CHUNK
        u = jnp.dot(xb, wup_ref[:, lo:hi], preferred_element_type=F32)
        v = jnp.dot(xb, wup_ref[:, D_FF + lo:D_FF + hi], preferred_element_type=F32)
        prev = carry_ref[:, lo:hi]
        p1 = jnp.broadcast_to(prev[7:8, :], u.shape)
        p2 = jnp.broadcast_to(prev[6:7, :], u.shape)
        u1 = jnp.where(rowi == 0, p1, pltpu.roll(u, 1, 0))
        u2 = jnp.where(rowi == 0, p2, jnp.where(rowi == 1, p1, pltpu.roll(u, 2, 0)))
        carry_ref[:, lo:hi] = u[tm - 8:tm, :]
        uc = cw_ref[0:1, lo:hi] * u2 + cw_ref[1:2, lo:hi] * u1 + cw_ref[2:3, lo:hi] * u
        h = (_gelu_tanh(uc) * v).astype(BF16)
        t = jnp.dot(h, wdn_ref[lo:hi, :], preferred_element_type=F32)
        acc = t if acc is None else acc + t
    ple = _sigmoid(jnp.dot(xb, wpg_ref[...], preferred_element_type=F32)) * jnp.dot(
        p_ref[0].astype(BF16), wpp_ref[...], preferred_element_type=F32)
    out = _layer_norm_rows(ALPHA * x32_ref[0] + acc + ple, lw_ref[...], lb_ref[...])
    o32_ref[0] = out
    o16_ref[0] = out.astype(BF16)


def _ffn_call(x32, xb, p, wup, convw, wdn, wpg, wpp, ln_w, ln_b):
    b, s, d = x32.shape
    tm = ROW_TILE
    row_spec = lambda width: pl.BlockSpec((1, tm, width), lambda i, t: (i, t, 0))
    full = lambda a: pl.BlockSpec(a.shape, lambda i, t: (0,) * a.ndim, pipeline_mode=pl.Buffered(1))
    lw = ln_w.reshape(1, d)
    lb = ln_b.reshape(1, d)
    body = functools.partial(_ffn_body, tm=tm)
    return pl.pallas_call(
        body,
        grid=(b, s // tm),
        in_specs=[row_spec(d), row_spec(d), row_spec(PLE_DIM)]
                 + [full(wup), full(convw), full(wdn), full(wpg), full(wpp), full(lw), full(lb)],
        out_specs=[row_spec(d), row_spec(d)],
        out_shape=[jax.ShapeDtypeStruct((b, s, d), F32), jax.ShapeDtypeStruct((b, s, d), BF16)],
        scratch_shapes=[pltpu.VMEM((8, D_FF), F32)],
        compiler_params=pltpu.CompilerParams(dimension_semantics=("parallel", "arbitrary"),
                                             vmem_limit_bytes=V7X_VMEM_LIMIT_BYTES),
        name="conv_ffn",
    )(x32, xb, p, wup, convw, wdn, wpg, wpp, lw, lb)


def _group_weights(w_in_l):
    offs = np.cumsum((0,) + GROUP_COLS)
    w_ret = w_in_l[:, offs[0]:offs[1]]
    w_rwkv = w_in_l[:, offs[1]:offs[2]]
    w_gla = w_in_l[:, offs[2]:offs[3]]
    w_hgrn = w_in_l[:, offs[3]:offs[4]]
    w_gate = w_in_l[:, offs[4]:offs[5]]
    perm = _ret_perm()
    nq = HEADS * RET_DK
    w_ret = jnp.concatenate([w_ret[:, perm], w_ret[:, nq + perm], w_ret[:, 2 * nq:]], axis=1)
    gq = HEADS * GLA_DK
    o_v, o_gl, o_g = 2 * gq, 2 * gq + BRANCH_WIDTH, 2 * gq + BRANCH_WIDTH + GLA_GATE_LORA
    pad = jnp.zeros((w_gla.shape[0], LANES - GLA_GATE_LORA), w_gla.dtype)
    w_gla = jnp.concatenate([w_gla[:, :o_gl], w_gla[:, o_g:], w_gla[:, o_gl:o_g], pad], axis=1)
    cast = lambda a: a.astype(BF16)
    return cast(w_ret), cast(w_rwkv), cast(w_gla), cast(w_hgrn), cast(w_gate)


def kernel(x, p, ln_in_w, ln_in_b, w_in, rwkv_mu, rwkv_w0, rwkv_w2, rwkv_a0, rwkv_a2, rwkv_g2, rwkv_k_k, rwkv_k_a, rwkv_r_k, rwkv_ln_w, rwkv_ln_b, gla_w2, gla_b, gla_norm_w, hgrn_lb_logits, hgrn_norm_w, w_branch, w_mix_out, ln_mix_w, ln_mix_b, w_ffn_up, ffn_conv, w_ffn_down, w_ple_gate, w_ple_proj, ln_ffn_w, ln_ffn_b):
    b, s, d = x.shape
    depth = w_in.shape[0]
    t = b * s
    v_head = np.arange(BRANCH_WIDTH) // HEAD_DV
    avg = jnp.asarray((v_head[:, None] == v_head[None, :]).astype(np.float32) / HEAD_DV).astype(BF16)
    rope = _rope_tables(s)

    x32, xb = _ln_call(x.reshape(t, d), ln_in_w, ln_in_b)
    for i in range(depth):
        w_ret, w_rwkv, w_gla, w_hgrn, w_gate = _group_weights(w_in[i])
        xb3 = xb.reshape(b, s, d)
        o_ret = _ret_call(xb3, w_ret, rope, avg)
        o_rwkv = _rwkv_call(xb3, w_rwkv, rwkv_mu[i], rwkv_w0[i], rwkv_w2[i], rwkv_a0[i], rwkv_a2[i],
                            rwkv_g2[i], rwkv_k_k[i], rwkv_k_a[i], rwkv_r_k[i], rwkv_ln_w[i], rwkv_ln_b[i], avg)
        w2p = jnp.concatenate([gla_w2[i], jnp.zeros((LANES - GLA_GATE_LORA, gla_w2.shape[2]), F32)], axis=0)
        o_gla = _gla_call(xb3, w_gla, w2p, gla_b[i].reshape(1, -1), jnp.tile(gla_norm_w[i], HEADS).reshape(1, -1), avg)
        o_hgrn = _hgrn_call(xb3, w_hgrn, hgrn_lb_logits.astype(F32), jnp.tile(hgrn_norm_w[i], HEADS).reshape(1, -1),
                            avg, i)
        branches = [o.reshape(t, BRANCH_WIDTH) for o in (o_ret, o_rwkv, o_gla, o_hgrn)]
        x32, xb = _merge_call(x32, xb, branches, w_gate, w_branch[i].astype(BF16), w_mix_out[i].astype(BF16),
                              ln_mix_w[i], ln_mix_b[i])
        x32, xb = _ffn_call(x32.reshape(b, s, d), xb.reshape(b, s, d), p[i], w_ffn_up[i].astype(BF16), ffn_conv[i],
                            w_ffn_down[i].astype(BF16), w_ple_gate[i].astype(BF16), w_ple_proj[i].astype(BF16),
                            ln_ffn_w[i], ln_ffn_b[i])
        x32 = x32.reshape(t, d)
        xb = xb.reshape(t, d)
    return x32.reshape(b, s, d)
```

```python
import functools
import math

import numpy as np
import jax
import jax.numpy as jnp
from jax import lax
from jax.experimental import pallas as pl
from jax.experimental.pallas import tpu as pltpu

F32 = jnp.float32
BF16 = jnp.bfloat16

D_MODEL = 1024
DEPTH = 2
N_BRANCH = 4
HEADS = 4
BRANCH_WIDTH = D_MODEL // 4
HEAD_DV = BRANCH_WIDTH // HEADS
RET_DK = HEAD_DV // 2
GLA_DK = HEAD_DV // 2
HGRN_DK = HEAD_DV
RWKV_N = HEAD_DV
RWKV_W_LORA = 64
RWKV_A_LORA = 64
RWKV_G_LORA = 128
GLA_GATE_LORA = 16
GLA_GATE_TEMP = 16.0
RET_ROPE_BASE = 10000.0
RWKV_DECAY_SCALE = 0.6065306597126334
D_FF = 2816
PLE_DIM = 256
LN_EPS = 1e-5
HEAD_EPS = 1e-6
RWKV_GN_EPS = 64e-5
ALPHA = (2.0 * DEPTH) ** 0.25

RET_COLS = (HEADS * RET_DK, HEADS * RET_DK, BRANCH_WIDTH, BRANCH_WIDTH)
RWKV_COLS = (BRANCH_WIDTH, BRANCH_WIDTH, BRANCH_WIDTH, RWKV_W_LORA, RWKV_A_LORA, RWKV_G_LORA)
GLA_COLS = (HEADS * GLA_DK, HEADS * GLA_DK, BRANCH_WIDTH, GLA_GATE_LORA, BRANCH_WIDTH)
HGRN_COLS = (HEADS * HGRN_DK, HEADS * HGRN_DK, BRANCH_WIDTH, BRANCH_WIDTH)
GROUP_COLS = (sum(RET_COLS), sum(RWKV_COLS), sum(GLA_COLS), sum(HGRN_COLS), N_BRANCH * D_MODEL)

V7X_VMEM_LIMIT_BYTES = 56 * 1024 * 1024
LANES = 128

TIME_TILE = 256
SEQ_PER_STEP = 4
RWKV_SEQ_PER_STEP = 4
RET_CHUNK = 128
GATED_CHUNK = 64
GATED_SUB = 8
RWKV_CHUNK = 64
ROW_TILE = 512
DENSE_SUB_ROWS = 256


_NN = (((1,), (0,)), ((), ()))
_NT = (((1,), (1,)), ((), ()))
_TN = (((0,), (0,)), ((), ()))


def _split(x, n):
    if x.dtype == BF16:
        return [x]
    parts = []
    r = x
    for i in range(n):
        p = r.astype(BF16)
        parts.append(p)
        if i + 1 < n:
            r = r - p.astype(F32)
    return parts


def _mm(a, b, dims=_NN, na=1, nb=1):
    ap = _split(a, na)
    bp = _split(b, nb)
    order = max(len(ap), len(bp))
    acc = None
    for i, x in enumerate(ap):
        for j, y in enumerate(bp):
            if i + j >= order:
                continue
            t = lax.dot_general(x, y, dims, preferred_element_type=F32)
            acc = t if acc is None else acc + t
    return acc


def _sigmoid(x):
    return 1.0 / (1.0 + jnp.exp(-x))


def _silu(x):
    return x * _sigmoid(x)


def _log_sigmoid(x):
    return jnp.minimum(x, 0.0) - jnp.log1p(jnp.exp(-jnp.abs(x)))


def _gelu_tanh(x):
    return 0.5 * x * (1.0 + jnp.tanh(math.sqrt(2.0 / math.pi) * (x + 0.044715 * (x * x * x))))


def _layer_norm_rows(h, w, b):
    mu = jnp.mean(h, axis=-1, keepdims=True)
    hc = h - mu
    var = jnp.mean(hc * hc, axis=-1, keepdims=True)
    return hc * lax.rsqrt(var + LN_EPS) * w + b


def _head_stack(x, head_of_lane):
    return jnp.concatenate([jnp.where(head_of_lane == h, x, 0.0) for h in range(HEADS)], axis=0)


def _head_unstack(y, head_of_lane, c):
    acc = None
    for h in range(HEADS):
        t = jnp.where(head_of_lane == h, y[h * c:(h + 1) * c], 0.0)
        acc = t if acc is None else acc + t
    return acc


def _seg_mean(x, avg):
    return _mm(x, avg)


def _round_robin(stage_generators):
    live = list(stage_generators)
    while live:
        live = [g for g in live if next(g, _DONE) is not _DONE]


_DONE = object()


def _lane_heads(width, per_head):
    return lax.broadcasted_iota(jnp.int32, (1, width), 1) // per_head


def _ln_body(x_ref, w_ref, b_ref, o32_ref, o16_ref):
    y = _layer_norm_rows(x_ref[...], w_ref[...], b_ref[...])
    o32_ref[...] = y
    o16_ref[...] = y.astype(BF16)


def _ln_call(x2, w, b):
    t, d = x2.shape
    return pl.pallas_call(
        _ln_body,
        grid=(t // ROW_TILE,),
        in_specs=[
            pl.BlockSpec((ROW_TILE, d), lambda i: (i, 0)),
            pl.BlockSpec((1, d), lambda i: (0, 0)),
            pl.BlockSpec((1, d), lambda i: (0, 0)),
        ],
        out_specs=[
            pl.BlockSpec((ROW_TILE, d), lambda i: (i, 0)),
            pl.BlockSpec((ROW_TILE, d), lambda i: (i, 0)),
        ],
        out_shape=[jax.ShapeDtypeStruct((t, d), F32), jax.ShapeDtypeStruct((t, d), BF16)],
        compiler_params=pltpu.CompilerParams(dimension_semantics=("parallel",)),
        name="ln_in",
    )(x2, w.reshape(1, d), b.reshape(1, d))


def _ret_body(x_ref, w_ref, cos_ref, sin_ref, dm_ref, xi_ref, zeta_ref, gc_ref, bd_ref, avg_ref,
              o_ref, z_scr, st_ref, *, bb, ts, c):
    @pl.when(pl.program_id(1) == 0)
    def _():
        st_ref[...] = jnp.zeros_like(st_ref)

    z_scr[...] = jnp.dot(x_ref[...].reshape(bb * ts, x_ref.shape[2]), w_ref[...], preferred_element_type=F32)
    hq = (lax.broadcasted_iota(jnp.int32, (1, 128), 1) % 64) // (RET_DK // 2)
    hv = _lane_heads(BRANCH_WIDTH, HEAD_DV)

    def one(ci, bi):
        trows = pl.ds(pl.multiple_of(ci * c, c), c)
        rows = pl.ds(pl.multiple_of(ci * c + bi * ts, c), c)
        zq = z_scr[rows, 0:128]
        zk = z_scr[rows, 128:256]
        v = z_scr[rows, 256:512]
        g = z_scr[rows, 512:768]
        cs = cos_ref[trows, :]
        sn = sin_ref[trows, :]
        q = zq * cs + pltpu.roll(zq, 64, 1) * sn
        k = (zk * cs + pltpu.roll(zk, 64, 1) * sn) * (RET_DK ** -0.5)
        yield
        sc = _mm(_head_stack(q, hq), k, _NT)
        r_state = st_ref[bi]
        cross = _mm(q * xi_ref[...], r_state)
        st_ref[bi] = r_state * gc_ref[...] + _mm(k, v * zeta_ref[...], _TN) * bd_ref[...]
        p = sc * dm_ref[...]
        yield
        inner = _head_unstack(_mm(p, v), hv, c)
        o = inner + cross
        yield
        mean = _seg_mean(o, avg_ref[...])
        d = o - mean
        yield
        var = _seg_mean(d * d, avg_ref[...])
        y = d * lax.rsqrt(var + HEAD_EPS)
        o_ref[bi, trows, :] = (_silu(g) * y).astype(o_ref.dtype)

    def chunk(ci, carry):
        _round_robin([one(ci, bi) for bi in range(bb)])
        return carry

    lax.fori_loop(0, ts // c, chunk, 0)


def _const_spec(shape):
    nd = len(shape)
    return pl.BlockSpec(shape, lambda b, t: (0,) * nd)


def _ret_tables(c):
    hs = np.arange(HEADS, dtype=np.float32)
    log_gamma = jnp.log1p(-jnp.exp2(-5.0 - jnp.asarray(hs)))
    pos = jnp.arange(c, dtype=F32)
    causal = pos[:, None] >= pos[None, :]
    rel = jnp.where(causal, pos[:, None] - pos[None, :], 0.0)
    decay = jnp.where(causal[None], jnp.exp(log_gamma[:, None, None] * rel[None]), 0.0)
    dm = decay.reshape(HEADS * c, c)
    q_head = (np.arange(128) % 64) // (RET_DK // 2)
    v_head = np.arange(BRANCH_WIDTH) // HEAD_DV
    xi = jnp.exp(log_gamma[None, :] * (pos[:, None] + 1.0))[:, q_head]
    zeta = jnp.exp(log_gamma[None, :] * (c - 1.0 - pos[:, None]))[:, v_head]
    gc = jnp.exp(log_gamma * c)[v_head][None, :]
    bd = jnp.asarray((q_head[:, None] == v_head[None, :]).astype(np.float32))
    return dm, xi, zeta, gc, bd


def _rope_tables(s):
    half = RET_DK // 2
    theta = 1.0 / (RET_ROPE_BASE ** jnp.linspace(0.0, 1.0, half))
    ang = jnp.arange(s, dtype=F32)[:, None] * theta[None, :]
    idx = np.arange(128) % half
    cos = jnp.cos(ang)[:, idx]
    sin = jnp.sin(ang)[:, idx]
    sign = jnp.asarray(np.where(np.arange(128) < 64, -1.0, 1.0).astype(np.float32))
    return cos, sin * sign[None, :]


def _ret_perm():
    half = RET_DK // 2
    l = np.arange(128)
    h = (l % 64) // half
    i = l % half
    return h * RET_DK + 2 * i + (l >= 64)


def _mixer_pallas(body, name, xb, w, args, arg_specs, state_shape, bb, extra_scratch=()):
    b, s, d = xb.shape
    ts = TIME_TILE
    return pl.pallas_call(
        functools.partial(body, bb=bb, ts=ts),
        grid=(b // bb, s // ts),
        in_specs=[pl.BlockSpec((bb, ts, d), lambda i, t: (i, t, 0)), _const_spec(w.shape)] + list(arg_specs),
        out_specs=pl.BlockSpec((bb, ts, BRANCH_WIDTH), lambda i, t: (i, t, 0)),
        out_shape=jax.ShapeDtypeStruct((b, s, BRANCH_WIDTH), BF16),
        scratch_shapes=[pltpu.VMEM((bb * ts, w.shape[1]), F32), pltpu.VMEM((bb,) + tuple(state_shape), F32)]
                       + list(extra_scratch),
        compiler_params=pltpu.CompilerParams(dimension_semantics=("parallel", "arbitrary"),
                                             vmem_limit_bytes=V7X_VMEM_LIMIT_BYTES),
        name=name,
    )(xb, w, *args)


def _ret_call(xb, w, s_tables, avg):
    ts, c = TIME_TILE, RET_CHUNK
    cos, sin = s_tables
    consts = _ret_tables(c) + (avg,)
    body = functools.partial(_ret_body, c=c)
    time_spec = pl.BlockSpec((ts, cos.shape[1]), lambda i, t: (t, 0))
    return _mixer_pallas(body, "retention", xb, w, (cos, sin) + consts,
                         [time_spec, time_spec] + [_const_spec(a.shape) for a in consts],
                         (HEADS * RET_DK, BRANCH_WIDTH), SEQ_PER_STEP)


_MASKED_EXPONENT = -1e30
_LOG2E = math.log2(math.e)


def _gated_chunk(q, k, v, glog, st_ref, tri, hexp, bdt, hk, hv, c, sub):
    w = q.shape[1]
    nb = c // sub
    vb = v.astype(BF16)
    bcum = _mm(tri, glog, na=1, nb=2) * _LOG2E
    yield
    st = st_ref[...]
    cross = _mm(q * jnp.exp2(bcum), st, _NT)
    blast = bcum[c - 1:c, :]
    kd = k * jnp.exp2(blast - bcum)
    st_ref[...] = st * jnp.exp2(blast) + _mm(v, kd, _TN) * bdt
    row = lax.broadcasted_iota(jnp.int32, (c, 1), 0)
    trow = lax.broadcasted_iota(jnp.int32, (sub, 1), 0)
    scores = []
    for i in range(1, nb):
        lo, hi = i * sub, (i + 1) * sub
        ref = bcum[lo - 1:lo, :]
        qt = q[lo:hi] * jnp.exp2(bcum[lo:hi] - ref)
        kt = k * jnp.exp2(jnp.where(row < lo, ref - bcum, _MASKED_EXPONENT))
        scores.append(_mm(_head_stack(qt, hk), kt, _NT))
    yield
    outs = [cross[0:sub]] + [cross[i * sub:(i + 1) * sub] + _head_unstack(_mm(scores[i - 1], vb), hv, sub)
                             for i in range(1, nb)]
    ps = []
    for i in range(nb):
        lo, hi = i * sub, (i + 1) * sub
        bi, qi, ki = bcum[lo:hi], q[lo:hi], k[lo:hi]
        for s in range(sub):
            bs = jnp.broadcast_to(bi[s:s + 1, :], (sub, w))
            ks = jnp.broadcast_to(ki[s:s + 1, :], (sub, w))
            e = jnp.exp2(jnp.where(trow >= s, bi - bs, _MASKED_EXPONENT))
            ps.append(qi * ks * e)
    rep = _mm(jnp.concatenate(ps, axis=0), hexp)
    yield
    for i in range(nb):
        vi = v[i * sub:(i + 1) * sub]
        acc = outs[i]
        for s in range(sub):
            r0 = (i * sub + s) * sub
            acc = acc + rep[r0:r0 + sub] * jnp.broadcast_to(vi[s:s + 1, :], (sub, BRANCH_WIDTH))
        outs[i] = acc
    return jnp.concatenate(outs, axis=0)


def _gla_body(x_ref, w_ref, w2_ref, b_ref, nw_ref, tri_ref, hexp_ref, bdt_ref, avg_ref,
              o_ref, z_scr, st_ref, *, bb, ts, c, sub):
    @pl.when(pl.program_id(1) == 0)
    def _():
        st_ref[...] = jnp.zeros_like(st_ref)

    z_scr[...] = jnp.dot(x_ref[...].reshape(bb * ts, x_ref.shape[2]), w_ref[...], preferred_element_type=F32)
    hk = _lane_heads(HEADS * GLA_DK, GLA_DK)
    hv = _lane_heads(BRANCH_WIDTH, HEAD_DV)

    def one(ci, bi):
        trows = pl.ds(pl.multiple_of(ci * c, c), c)
        rows = pl.ds(pl.multiple_of(ci * c + bi * ts, c), c)
        q = z_scr[rows, 0:128] * (GLA_DK ** -0.5)
        k = z_scr[rows, 128:256]
        v = z_scr[rows, 256:512]
        g = z_scr[rows, 512:768]
        gl = z_scr[rows, 768:896]
        glog = _log_sigmoid(_mm(gl, w2_ref[...]) + b_ref[...]) / GLA_GATE_TEMP
        yield
        o = yield from _gated_chunk(q, k, v, glog, st_ref.at[bi], tri_ref[...], hexp_ref[...], bdt_ref[...],
                                    hk, hv, c, sub)
        yield
        ms = _seg_mean(o * o, avg_ref[...])
        y = o * lax.rsqrt(ms + HEAD_EPS) * nw_ref[...]
        o_ref[bi, trows, :] = (y * _silu(g)).astype(o_ref.dtype)

    def chunk(ci, carry):
        _round_robin([one(ci, bi) for bi in range(bb)])
        return carry

    lax.fori_loop(0, ts // c, chunk, 0)


def _hgrn_body(x_ref, w_ref, lbl_ref, nw_ref, tri_ref, hexp_ref, bdt_ref, avg_ref,
               o_ref, z_scr, st_ref, *, bb, ts, c, sub, layer):
    @pl.when(pl.program_id(1) == 0)
    def _():
        st_ref[...] = jnp.zeros_like(st_ref)

    z_scr[...] = jnp.dot(x_ref[...].reshape(bb * ts, x_ref.shape[2]), w_ref[...], preferred_element_type=F32)
    hk = _lane_heads(HEADS * HGRN_DK, HGRN_DK)
    hv = _lane_heads(BRANCH_WIDTH, HEAD_DV)
    logits = lbl_ref[...]
    ex = jnp.exp(logits - jnp.max(logits, axis=0, keepdims=True))
    prob = ex / jnp.sum(ex, axis=0, keepdims=True)
    lb = jnp.zeros((1, HEADS * HGRN_DK), F32)
    for j in range(1, layer + 1):
        lb = lb + prob[j:j + 1, :]

    def one(ci, bi):
        trows = pl.ds(pl.multiple_of(ci * c, c), c)
        rows = pl.ds(pl.multiple_of(ci * c + bi * ts, c), c)
        q = z_scr[rows, 0:256]
        fz = z_scr[rows, 256:512]
        v = z_scr[rows, 512:768]
        g = z_scr[rows, 768:1024]
        f = lb + (1.0 - lb) * _sigmoid(fz)
        glog = jnp.log(f)
        k = (1.0 - lb) * _sigmoid(-fz)
        o = yield from _gated_chunk(q, k, v, glog, st_ref.at[bi], tri_ref[...], hexp_ref[...], bdt_ref[...],
                                    hk, hv, c, sub)
        yield
        ms = _seg_mean(o * o, avg_ref[...])
        y = o * lax.rsqrt(ms + HEAD_EPS) * nw_ref[...]
        o_ref[bi, trows, :] = (y * _silu(g)).astype(o_ref.dtype)

    def chunk(ci, carry):
        _round_robin([one(ci, bi) for bi in range(bb)])
        return carry

    lax.fori_loop(0, ts // c, chunk, 0)


def _gated_consts(wk, dk, c):
    tri = jnp.asarray(np.tril(np.ones((c, c), np.float32))).astype(BF16)
    k_head = np.arange(wk) // dk
    v_head = np.arange(BRANCH_WIDTH) // HEAD_DV
    same = (k_head[:, None] == v_head[None, :]).astype(np.float32)
    hexp = jnp.asarray(same).astype(BF16)
    bdt = jnp.asarray(same.T)
    return tri, hexp, bdt


def _gla_call(xb, w, w2p, bias, norm_w, avg):
    wk = HEADS * GLA_DK
    args = (w2p, bias, norm_w) + _gated_consts(wk, GLA_DK, GATED_CHUNK) + (avg,)
    body = functools.partial(_gla_body, c=GATED_CHUNK, sub=GATED_SUB)
    return _mixer_pallas(body, "gla", xb, w, args, [_const_spec(a.shape) for a in args], (BRANCH_WIDTH, wk),
                         SEQ_PER_STEP)


def _hgrn_call(xb, w, lb_logits, norm_w, avg, layer):
    wk = HEADS * HGRN_DK
    args = (lb_logits, norm_w) + _gated_consts(wk, HGRN_DK, GATED_CHUNK) + (avg,)
    body = functools.partial(_hgrn_body, c=GATED_CHUNK, sub=GATED_SUB, layer=layer)
    return _mixer_pallas(body, "hgrn2", xb, w, args, [_const_spec(a.shape) for a in args], (BRANCH_WIDTH, wk),
                         SEQ_PER_STEP)


def _rwkv_body(x_ref, w_ref, mu_ref, w0_ref, w2_ref, a0_ref, a2_ref, g2_ref, kk_ref, ka_ref, rk_ref,
               lnw_ref, lnb_ref, tri_ref, ms_ref, mi_ref, eye_ref, avg_ref,
               o_ref, z_scr, st_ref, last_ref, *, bb, ts, c):
    @pl.when(pl.program_id(1) == 0)
    def _():
        st_ref[...] = jnp.zeros_like(st_ref)
        last_ref[...] = jnp.zeros_like(last_ref)

    rowi = lax.broadcasted_iota(jnp.int32, (ts, 1), 0)
    for bi in range(bb):
        z = jnp.dot(x_ref[bi], w_ref[...], preferred_element_type=F32)
        prev = jnp.where(rowi == 0, jnp.broadcast_to(last_ref[bi, 7:8, :], z.shape), pltpu.roll(z, 1, 0))
        last_ref[bi] = z[ts - 8:ts, :]
        z_scr[bi * ts:(bi + 1) * ts, :] = z + (prev - z) * mu_ref[...]

    hl = _lane_heads(BRANCH_WIDTH, RWKV_N)
    mm = _mm

    def one(ci, bi):
        trows = pl.ds(pl.multiple_of(ci * c, c), c)
        rows = pl.ds(pl.multiple_of(ci * c + bi * ts, c), c)
        r = z_scr[rows, 0:256]
        k = z_scr[rows, 256:512]
        v = z_scr[rows, 512:768]
        wl = z_scr[rows, 768:832]
        al = z_scr[rows, 832:896]
        gl = z_scr[rows, 896:1024]
        lw = -RWKV_DECAY_SCALE * _sigmoid(w0_ref[...] + _mm(jnp.tanh(wl), w2_ref[...], na=2, nb=2))
        a = _sigmoid(a0_ref[...] + _mm(al, a2_ref[...]))
        g = _mm(_sigmoid(gl), g2_ref[...])
        kk = k * kk_ref[...]
        kk = kk * lax.rsqrt(_mm(kk * kk, avg_ref[...]) * float(RWKV_N) + 1e-12)
        k2 = k * (1.0 + (a - 1.0) * ka_ref[...])
        beta = kk * a
        cw = _mm(tri_ref[...], lw, na=1, nb=2)
        e_in = jnp.exp(cw)
        e_out = jnp.exp(-cw)
        at = -kk * jnp.exp(cw - lw)
        rt = r * e_in
        bt = beta * e_out
        kt = k2 * e_out
        n = HEADS * c
        hs_a = _head_stack(at, hl).astype(BF16)
        hs_r = _head_stack(rt, hl)
        lhs_ar = jnp.concatenate([hs_a, hs_r.astype(BF16)], axis=0)
        rhs_bk = jnp.concatenate([bt] * HEADS + [kt] * HEADS, axis=0).astype(BF16)
        yield
        gram = mm(lhs_ar, rhs_bk, _NT)
        a_ab = gram[0:n, 0:n] * ms_ref[...]
        a_ak = (gram[0:n, n:2 * n] * ms_ref[...]).astype(BF16)
        a_rb = (gram[n:2 * n, 0:n] * mi_ref[...]).astype(BF16)
        a_rk = (gram[n:2 * n, n:2 * n] * mi_ref[...]).astype(BF16)
        tinv = eye_ref[...] + a_ab
        pw = a_ab.astype(BF16)
        yield
        pw = mm(pw, pw).astype(BF16)
        v_st = _head_stack(v, hl).astype(BF16)
        akrk_v = mm(jnp.concatenate([a_ak, a_rk], axis=0), v_st)
        av = akrk_v[0:n].astype(BF16)
        for _ in range(int(math.log2(c)) - 2):
            yield
            both = mm(jnp.concatenate([pw, tinv.astype(BF16)], axis=0), pw)
            tinv = tinv + both[n:2 * n]
            pw = both[0:n].astype(BF16)
        yield
        tinv = (tinv + mm(tinv, pw)).astype(BF16)
        yield
        w12 = mm(tinv, jnp.concatenate([av, hs_a], axis=1)).astype(BF16)
        yield
        aw = mm(a_rb, w12)
        cwl = cw[c - 1:c, :]
        dec = jnp.exp(cwl - cw)
        hs_bh = _head_stack(beta * dec, hl).astype(BF16)
        hs_kh = _head_stack(k2 * dec, hl).astype(BF16)
        bw = mm(hs_bh, w12, _TN)
        yield
        st = st_ref[bi].astype(BF16)
        wv = BRANCH_WIDTH
        y_st = mm(hs_r + aw[:, wv:2 * wv], st) + aw[:, 0:wv] + akrk_v[n:2 * n]
        y = y_st[0:c] + y_st[c:2 * c] + y_st[2 * c:3 * c] + y_st[3 * c:4 * c]
        m_t = eye_ref[...] * jnp.exp(cwl) + bw[:, wv:2 * wv]
        st_ref[bi] = mm(m_t, st) + bw[:, 0:wv] + mm(hs_kh, v_st, _TN)
        yield
        mean = _seg_mean(y, avg_ref[...])
        d = y - mean
        var = _seg_mean(d * d, avg_ref[...])
        yn = d * lax.rsqrt(var + RWKV_GN_EPS) * lnw_ref[...] + lnb_ref[...]
        bonus = _mm(r * k2 * rk_ref[...], avg_ref[...]) * float(RWKV_N) * v
        o_ref[bi, trows, :] = ((yn + bonus) * g).astype(o_ref.dtype)

    def chunk(ci, carry):
        _round_robin([one(ci, bi) for bi in range(bb)])
        return carry

    lax.fori_loop(0, ts // c, chunk, 0)


def _rwkv_call(xb, w, mu, w0, w2, a0, a2, g2, k_k, k_a, r_k, ln_w, ln_b, avg):
    b, s, d = xb.shape
    ts, c = TIME_TILE, RWKV_CHUNK
    n = HEADS * c
    tri = jnp.asarray(np.tril(np.ones((c, c), np.float32))).astype(BF16)
    ridx = np.arange(n)
    same = (ridx[:, None] // c) == (ridx[None, :] // c)
    ms = jnp.asarray((same & ((ridx[None, :] % c) < (ridx[:, None] % c))).astype(np.float32))
    mi = jnp.asarray((same & ((ridx[None, :] % c) <= (ridx[:, None] % c))).astype(np.float32))
    eye = jnp.asarray(np.eye(n, dtype=np.float32))
    row = lambda a: a.reshape(1, -1).astype(F32)
    args = (row(mu), row(w0), w2.astype(F32), row(a0), a2.astype(F32), g2.astype(F32),
            row(k_k), row(k_a), row(r_k), row(ln_w), row(ln_b), tri, ms, mi, eye, avg)
    body = functools.partial(_rwkv_body, c=c)
    return _mixer_pallas(body, "rwkv7", xb, w, args, [_const_spec(a.shape) for a in args], (n, BRANCH_WIDTH),
                         RWKV_SEQ_PER_STEP, extra_scratch=[pltpu.VMEM((RWKV_SEQ_PER_STEP, 8, w.shape[1]), F32)])


def _merge_body(x32_ref, xb_ref, b0_ref, b1_ref, b2_ref, b3_ref, wg_ref, wb_ref, wm_ref, lw_ref, lb_ref,
                o32_ref, o16_ref):
    tm = x32_ref.shape[0]
    for r0 in range(0, tm, DENSE_SUB_ROWS):
        rows = slice(r0, r0 + DENSE_SUB_ROWS)
        xb = xb_ref[rows, :]
        acc = None
        for n, br_ref in enumerate((b0_ref, b1_ref, b2_ref, b3_ref)):
            gate = jnp.dot(xb, wg_ref[:, n * D_MODEL:(n + 1) * D_MODEL], preferred_element_type=F32)
            br = jnp.dot(br_ref[rows, :], wb_ref[n], preferred_element_type=F32)
            t = _sigmoid(gate) * br
            acc = t if acc is None else acc + t
        y = jnp.dot(acc.astype(BF16), wm_ref[...], preferred_element_type=F32)
        out = _layer_norm_rows(ALPHA * x32_ref[rows, :] + y, lw_ref[...], lb_ref[...])
        o32_ref[rows, :] = out
        o16_ref[rows, :] = out.astype(BF16)


def _merge_call(x32, xb, branches, wg, wb, wm, ln_w, ln_b):
    t, d = x32.shape
    tm = ROW_TILE
    row_spec = lambda width: pl.BlockSpec((tm, width), lambda i: (i, 0))
    full = lambda a: pl.BlockSpec(a.shape, lambda i: (0,) * a.ndim)
    lw = ln_w.reshape(1, d)
    lb = ln_b.reshape(1, d)
    return pl.pallas_call(
        _merge_body,
        grid=(t // tm,),
        in_specs=[row_spec(d), row_spec(d)] + [row_spec(BRANCH_WIDTH)] * N_BRANCH
                 + [full(wg), full(wb), full(wm), full(lw), full(lb)],
        out_specs=[row_spec(d), row_spec(d)],
        out_shape=[jax.ShapeDtypeStruct((t, d), F32), jax.ShapeDtypeStruct((t, d), BF16)],
        compiler_params=pltpu.CompilerParams(dimension_semantics=("parallel",),
                                             vmem_limit_bytes=V7X_VMEM_LIMIT_BYTES),
        name="merge",
    )(x32, xb, *branches, wg, wb, wm, lw, lb)


FFN_CHUNK = 1408


def _ffn_body(x32_ref, xb_ref, p_ref, wup_ref, cw_ref, wdn_ref, wpg_ref, wpp_ref, lw_ref, lb_ref,
              o32_ref, o16_ref, carry_ref, *, tm):
    @pl.when(pl.program_id(1) == 0)
    def _():
        carry_ref[...] = jnp.zeros_like(carry_ref)

    xb = xb_ref[0]
    rowi = lax.broadcasted_iota(jnp.int32, (tm, 1), 0)
    acc = None
    for ci in range(D_FF // FFN_CHUNK):
        lo, hi = ci * FFN_CHUNK, (ci + 1) * FFN_CHUNK
        u = jnp.dot(xb, wup_ref[:, lo:hi], preferred_element_type=F32)
        v = jnp.dot(xb, wup_ref[:, D_FF + lo:D_FF + hi], preferred_element_type=F32)
        prev = carry_ref[:, lo:hi]
        p1 = jnp.broadcast_to(prev[7:8, :], u.shape)
        p2 = jnp.broadcast_to(prev[6:7, :], u.shape)
        u1 = jnp.where(rowi == 0, p1, pltpu.roll(u, 1, 0))
        u2 = jnp.where(rowi == 0, p2, jnp.where(rowi == 1, p1, pltpu.roll(u, 2, 0)))
        carry_ref[:, lo:hi] = u[tm - 8:tm, :]
        uc = cw_ref[0:1, lo:hi] * u2 + cw_ref[1:2, lo:hi] * u1 + cw_ref[2:3, lo:hi] * u
        h = (_gelu_tanh(uc) * v).astype(BF16)
        t = jnp.dot(h, wdn_ref[lo:hi, :], preferred_element_type=F32)
        acc = t if acc is None else acc + t
    ple = _sigmoid(jnp.dot(xb, wpg_ref[...], preferred_element_type=F32)) * jnp.dot(
        p_ref[0].astype(BF16), wpp_ref[...], preferred_element_type=F32)
    out = _layer_norm_rows(ALPHA * x32_ref[0] + acc + ple, lw_ref[...], lb_ref[...])
    o32_ref[0] = out
    o16_ref[0] = out.astype(BF16)


def _ffn_call(x32, xb, p, wup, convw, wdn, wpg, wpp, ln_w, ln_b):
    b, s, d = x32.shape
    tm = ROW_TILE
    row_spec = lambda width: pl.BlockSpec((1, tm, width), lambda i, t: (i, t, 0))
    full = lambda a: pl.BlockSpec(a.shape, lambda i, t: (0,) * a.ndim, pipeline_mode=pl.Buffered(1))
    lw = ln_w.reshape(1, d)
    lb = ln_b.reshape(1, d)
    body = functools.partial(_ffn_body, tm=tm)
    return pl.pallas_call(
        body,
        grid=(b, s // tm),
        in_specs=[row_spec(d), row_spec(d), row_spec(PLE_DIM)]
                 + [full(wup), full(convw), full(wdn), full(wpg), full(wpp), full(lw), full(lb)],
        out_specs=[row_spec(d), row_spec(d)],
        out_shape=[jax.ShapeDtypeStruct((b, s, d), F32), jax.ShapeDtypeStruct((b, s, d), BF16)],
        scratch_shapes=[pltpu.VMEM((8, D_FF), F32)],
        compiler_params=pltpu.CompilerParams(dimension_semantics=("parallel", "arbitrary"),
                                             vmem_limit_bytes=V7X_VMEM_LIMIT_BYTES),
        name="conv_ffn",
    )(x32, xb, p, wup, convw, wdn, wpg, wpp, lw, lb)


def _group_weights(w_in_l):
    offs = np.cumsum((0,) + GROUP_COLS)
    w_ret = w_in_l[:, offs[0]:offs[1]]
    w_rwkv = w_in_l[:, offs[1]:offs[2]]
    w_gla = w_in_l[:, offs[2]:offs[3]]
    w_hgrn = w_in_l[:, offs[3]:offs[4]]
    w_gate = w_in_l[:, offs[4]:offs[5]]
    perm = _ret_perm()
    nq = HEADS * RET_DK
    w_ret = jnp.concatenate([w_ret[:, perm], w_ret[:, nq + perm], w_ret[:, 2 * nq:]], axis=1)
    gq = HEADS * GLA_DK
    o_v, o_gl, o_g = 2 * gq, 2 * gq + BRANCH_WIDTH, 2 * gq + BRANCH_WIDTH + GLA_GATE_LORA
    pad = jnp.zeros((w_gla.shape[0], LANES - GLA_GATE_LORA), w_gla.dtype)
    w_gla = jnp.concatenate([w_gla[:, :o_gl], w_gla[:, o_g:], w_gla[:, o_gl:o_g], pad], axis=1)
    cast = lambda a: a.astype(BF16)
    return cast(w_ret), cast(w_rwkv), cast(w_gla), cast(w_hgrn), cast(w_gate)


def kernel(x, p, ln_in_w, ln_in_b, w_in, rwkv_mu, rwkv_w0, rwkv_w2, rwkv_a0, rwkv_a2, rwkv_g2, rwkv_k_k, rwkv_k_a, rwkv_r_k, rwkv_ln_w, rwkv_ln_b, gla_w2, gla_b, gla_norm_w, hgrn_lb_logits, hgrn_norm_w, w_branch, w_mix_out, ln_mix_w, ln_mix_b, w_ffn_up, ffn_conv, w_ffn_down, w_ple_gate, w_ple_proj, ln_ffn_w, ln_ffn_b):
    b, s, d = x.shape
    depth = w_in.shape[0]
    t = b * s
    v_head = np.arange(BRANCH_WIDTH) // HEAD_DV
    avg = jnp.asarray((v_head[:, None] == v_head[None, :]).astype(np.float32) / HEAD_DV).astype(BF16)
    rope = _rope_tables(s)

    x32, xb = _ln_call(x.reshape(t, d), ln_in_w, ln_in_b)
    for i in range(depth):
        w_ret, w_rwkv, w_gla, w_hgrn, w_gate = _group_weights(w_in[i])
        xb3 = xb.reshape(b, s, d)
        o_ret = _ret_call(xb3, w_ret, rope, avg)
        o_rwkv = _rwkv_call(xb3, w_rwkv, rwkv_mu[i], rwkv_w0[i], rwkv_w2[i], rwkv_a0[i], rwkv_a2[i],
                            rwkv_g2[i], rwkv_k_k[i], rwkv_k_a[i], rwkv_r_k[i], rwkv_ln_w[i], rwkv_ln_b[i], avg)
        w2p = jnp.concatenate([gla_w2[i], jnp.zeros((LANES - GLA_GATE_LORA, gla_w2.shape[2]), F32)], axis=0)
        o_gla = _gla_call(xb3, w_gla, w2p, gla_b[i].reshape(1, -1), jnp.tile(gla_norm_w[i], HEADS).reshape(1, -1), avg)
        o_hgrn = _hgrn_call(xb3, w_hgrn, hgrn_lb_logits.astype(F32), jnp.tile(hgrn_norm_w[i], HEADS).reshape(1, -1),
                            avg, i)
        branches = [o.reshape(t, BRANCH_WIDTH) for o in (o_ret, o_rwkv, o_gla, o_hgrn)]
        x32, xb = _merge_call(x32, xb, branches, w_gate, w_branch[i].astype(BF16), w_mix_out[i].astype(BF16),
                              ln_mix_w[i], ln_mix_b[i])
        x32, xb = _ffn_call(x32.reshape(b, s, d), xb.reshape(b, s, d), p[i], w_ffn_up[i].astype(BF16), ffn_conv[i],
                            w_ffn_down[i].astype(BF16), w_ple_gate[i].astype(BF16), w_ple_proj[i].astype(BF16),
                            ln_ffn_w[i], ln_ffn_b[i])
        x32 = x32.reshape(t, d)
        xb = xb.reshape(t, d)
    return x32.reshape(b, s, d)
```

```python
import functools
import math

import numpy as np
import jax
import jax.numpy as jnp
from jax import lax
from jax.experimental import pallas as pl
from jax.experimental.pallas import tpu as pltpu

F32 = jnp.float32
BF16 = jnp.bfloat16

D_MODEL = 1024
DEPTH = 2
N_BRANCH = 4
HEADS = 4
BRANCH_WIDTH = D_MODEL // 4
HEAD_DV = BRANCH_WIDTH // HEADS
RET_DK = HEAD_DV // 2
GLA_DK = HEAD_DV // 2
HGRN_DK = HEAD_DV
RWKV_N = HEAD_DV
RWKV_W_LORA = 64
RWKV_A_LORA = 64
RWKV_G_LORA = 128
GLA_GATE_LORA = 16
GLA_GATE_TEMP = 16.0
RET_ROPE_BASE = 10000.0
RWKV_DECAY_SCALE = 0.6065306597126334
D_FF = 2816
PLE_DIM = 256
LN_EPS = 1e-5
HEAD_EPS = 1e-6
RWKV_GN_EPS = 64e-5
ALPHA = (2.0 * DEPTH) ** 0.25

RET_COLS = (HEADS * RET_DK, HEADS * RET_DK, BRANCH_WIDTH, BRANCH_WIDTH)
RWKV_COLS = (BRANCH_WIDTH, BRANCH_WIDTH, BRANCH_WIDTH, RWKV_W_LORA, RWKV_A_LORA, RWKV_G_LORA)
GLA_COLS = (HEADS * GLA_DK, HEADS * GLA_DK, BRANCH_WIDTH, GLA_GATE_LORA, BRANCH_WIDTH)
HGRN_COLS = (HEADS * HGRN_DK, HEADS * HGRN_DK, BRANCH_WIDTH, BRANCH_WIDTH)
GROUP_COLS = (sum(RET_COLS), sum(RWKV_COLS), sum(GLA_COLS), sum(HGRN_COLS), N_BRANCH * D_MODEL)

V7X_VMEM_LIMIT_BYTES = 56 * 1024 * 1024
LANES = 128

TIME_TILE = 256
SEQ_PER_STEP = 4
RWKV_SEQ_PER_STEP = 4
RET_CHUNK = 128
GATED_CHUNK = 64
GATED_SUB = 8
RWKV_CHUNK = 64
ROW_TILE = 512
DENSE_SUB_ROWS = 256


_NN = (((1,), (0,)), ((), ()))
_NT = (((1,), (1,)), ((), ()))
_TN = (((0,), (0,)), ((), ()))


def _split(x, n):
    if x.dtype == BF16:
        return [x]
    parts = []
    r = x
    for i in range(n):
        p = r.astype(BF16)
        parts.append(p)
        if i + 1 < n:
            r = r - p.astype(F32)
    return parts


def _mm(a, b, dims=_NN, na=1, nb=1):
    ap = _split(a, na)
    bp = _split(b, nb)
    order = max(len(ap), len(bp))
    acc = None
    for i, x in enumerate(ap):
        for j, y in enumerate(bp):
            if i + j >= order:
                continue
            t = lax.dot_general(x, y, dims, preferred_element_type=F32)
            acc = t if acc is None else acc + t
    return acc


def _sigmoid(x):
    return 1.0 / (1.0 + jnp.exp(-x))


def _silu(x):
    return x * _sigmoid(x)


def _log_sigmoid(x):
    return jnp.minimum(x, 0.0) - jnp.log1p(jnp.exp(-jnp.abs(x)))


def _gelu_tanh(x):
    return 0.5 * x * (1.0 + jnp.tanh(math.sqrt(2.0 / math.pi) * (x + 0.044715 * (x * x * x))))


def _layer_norm_rows(h, w, b):
    mu = jnp.mean(h, axis=-1, keepdims=True)
    hc = h - mu
    var = jnp.mean(hc * hc, axis=-1, keepdims=True)
    return hc * lax.rsqrt(var + LN_EPS) * w + b


def _head_stack(x, head_of_lane):
    return jnp.concatenate([jnp.where(head_of_lane == h, x, 0.0) for h in range(HEADS)], axis=0)


def _head_unstack(y, head_of_lane, c):
    acc = None
    for h in range(HEADS):
        t = jnp.where(head_of_lane == h, y[h * c:(h + 1) * c], 0.0)
        acc = t if acc is None else acc + t
    return acc


def _seg_mean(x, avg):
    return _mm(x, avg)


def _round_robin(stage_generators):
    live = list(stage_generators)
    while live:
        live = [g for g in live if next(g, _DONE) is not _DONE]


_DONE = object()


def _lane_heads(width, per_head):
    return lax.broadcasted_iota(jnp.int32, (1, width), 1) // per_head


def _ln_body(x_ref, w_ref, b_ref, o32_ref, o16_ref):
    y = _layer_norm_rows(x_ref[...], w_ref[...], b_ref[...])
    o32_ref[...] = y
    o16_ref[...] = y.astype(BF16)


def _ln_call(x2, w, b):
    t, d = x2.shape
    return pl.pallas_call(
        _ln_body,
        grid=(t // ROW_TILE,),
        in_specs=[
            pl.BlockSpec((ROW_TILE, d), lambda i: (i, 0)),
            pl.BlockSpec((1, d), lambda i: (0, 0)),
            pl.BlockSpec((1, d), lambda i: (0, 0)),
        ],
        out_specs=[
            pl.BlockSpec((ROW_TILE, d), lambda i: (i, 0)),
            pl.BlockSpec((ROW_TILE, d), lambda i: (i, 0)),
        ],
        out_shape=[jax.ShapeDtypeStruct((t, d), F32), jax.ShapeDtypeStruct((t, d), BF16)],
        compiler_params=pltpu.CompilerParams(dimension_semantics=("parallel",)),
        name="ln_in",
    )(x2, w.reshape(1, d), b.reshape(1, d))


def _ret_body(x_ref, w_ref, cos_ref, sin_ref, dm_ref, xi_ref, zeta_ref, gc_ref, bd_ref, avg_ref,
              o_ref, z_scr, st_ref, *, bb, ts, c):
    @pl.when(pl.program_id(1) == 0)
    def _():
        st_ref[...] = jnp.zeros_like(st_ref)

    z_scr[...] = jnp.dot(x_ref[...].reshape(bb * ts, x_ref.shape[2]), w_ref[...], preferred_element_type=F32)
    hq = (lax.broadcasted_iota(jnp.int32, (1, 128), 1) % 64) // (RET_DK // 2)
    hv = _lane_heads(BRANCH_WIDTH, HEAD_DV)

    def one(ci, bi):
        trows = pl.ds(pl.multiple_of(ci * c, c), c)
        rows = pl.ds(pl.multiple_of(ci * c + bi * ts, c), c)
        zq = z_scr[rows, 0:128]
        zk = z_scr[rows, 128:256]
        v = z_scr[rows, 256:512]
        g = z_scr[rows, 512:768]
        cs = cos_ref[trows, :]
        sn = sin_ref[trows, :]
        q = zq * cs + pltpu.roll(zq, 64, 1) * sn
        k = (zk * cs + pltpu.roll(zk, 64, 1) * sn) * (RET_DK ** -0.5)
        yield
        sc = _mm(_head_stack(q, hq), k, _NT)
        r_state = st_ref[bi]
        cross = _mm(q * xi_ref[...], r_state)
        st_ref[bi] = r_state * gc_ref[...] + _mm(k, v * zeta_ref[...], _TN) * bd_ref[...]
        p = sc * dm_ref[...]
        yield
        inner = _head_unstack(_mm(p, v), hv, c)
        o = inner + cross
        yield
        mean = _seg_mean(o, avg_ref[...])
        d = o - mean
        yield
        var = _seg_mean(d * d, avg_ref[...])
        y = d * lax.rsqrt(var + HEAD_EPS)
        o_ref[bi, trows, :] = (_silu(g) * y).astype(o_ref.dtype)

    def chunk(ci, carry):
        _round_robin([one(ci, bi) for bi in range(bb)])
        return carry

    lax.fori_loop(0, ts // c, chunk, 0)


def _const_spec(shape):
    nd = len(shape)
    return pl.BlockSpec(shape, lambda b, t: (0,) * nd)


def _ret_tables(c):
    hs = np.arange(HEADS, dtype=np.float32)
    log_gamma = jnp.log1p(-jnp.exp2(-5.0 - jnp.asarray(hs)))
    pos = jnp.arange(c, dtype=F32)
    causal = pos[:, None] >= pos[None, :]
    rel = jnp.where(causal, pos[:, None] - pos[None, :], 0.0)
    decay = jnp.where(causal[None], jnp.exp(log_gamma[:, None, None] * rel[None]), 0.0)
    dm = decay.reshape(HEADS * c, c)
    q_head = (np.arange(128) % 64) // (RET_DK // 2)
    v_head = np.arange(BRANCH_WIDTH) // HEAD_DV
    xi = jnp.exp(log_gamma[None, :] * (pos[:, None] + 1.0))[:, q_head]
    zeta = jnp.exp(log_gamma[None, :] * (c - 1.0 - pos[:, None]))[:, v_head]
    gc = jnp.exp(log_gamma * c)[v_head][None, :]
    bd = jnp.asarray((q_head[:, None] == v_head[None, :]).astype(np.float32))
    return dm, xi, zeta, gc, bd


def _rope_tables(s):
    half = RET_DK // 2
    theta = 1.0 / (RET_ROPE_BASE ** jnp.linspace(0.0, 1.0, half))
    ang = jnp.arange(s, dtype=F32)[:, None] * theta[None, :]
    idx = np.arange(128) % half
    cos = jnp.cos(ang)[:, idx]
    sin = jnp.sin(ang)[:, idx]
    sign = jnp.asarray(np.where(np.arange(128) < 64, -1.0, 1.0).astype(np.float32))
    return cos, sin * sign[None, :]


def _ret_perm():
    half = RET_DK // 2
    l = np.arange(128)
    h = (l % 64) // half
    i = l % half
    return h * RET_DK + 2 * i + (l >= 64)


def _mixer_pallas(body, name, xb, w, args, arg_specs, state_shape, bb, extra_scratch=()):
    b, s, d = xb.shape
    ts = TIME_TILE
    return pl.pallas_call(
        functools.partial(body, bb=bb, ts=ts),
        grid=(b // bb, s // ts),
        in_specs=[pl.BlockSpec((bb, ts, d), lambda i, t: (i, t, 0)), _const_spec(w.shape)] + list(arg_specs),
        out_specs=pl.BlockSpec((bb, ts, BRANCH_WIDTH), lambda i, t: (i, t, 0)),
        out_shape=jax.ShapeDtypeStruct((b, s, BRANCH_WIDTH), BF16),
        scratch_shapes=[pltpu.VMEM((bb * ts, w.shape[1]), F32), pltpu.VMEM((bb,) + tuple(state_shape), F32)]
                       + list(extra_scratch),
        compiler_params=pltpu.CompilerParams(dimension_semantics=("parallel", "arbitrary"),
                                             vmem_limit_bytes=V7X_VMEM_LIMIT_BYTES),
        name=name,
    )(xb, w, *args)


def _ret_call(xb, w, s_tables, avg):
    ts, c = TIME_TILE, RET_CHUNK
    cos, sin = s_tables
    consts = _ret_tables(c) + (avg,)
    body = functools.partial(_ret_body, c=c)
    time_spec = pl.BlockSpec((ts, cos.shape[1]), lambda i, t: (t, 0))
    return _mixer_pallas(body, "retention", xb, w, (cos, sin) + consts,
                         [time_spec, time_spec] + [_const_spec(a.shape) for a in consts],
                         (HEADS * RET_DK, BRANCH_WIDTH), SEQ_PER_STEP)


_MASKED_EXPONENT = -1e30
_LOG2E = math.log2(math.e)


def _gated_chunk(q, k, v, glog, st_ref, tri, hexp, bdt, hk, hv, c, sub):
    w = q.shape[1]
    nb = c // sub
    vb = v.astype(BF16)
    bcum = _mm(tri, glog, na=1, nb=2) * _LOG2E
    yield
    st = st_ref[...]
    cross = _mm(q * jnp.exp2(bcum), st, _NT)
    blast = bcum[c - 1:c, :]
    kd = k * jnp.exp2(blast - bcum)
    st_ref[...] = st * jnp.exp2(blast) + _mm(v, kd, _TN) * bdt
    row = lax.broadcasted_iota(jnp.int32, (c, 1), 0)
    trow = lax.broadcasted_iota(jnp.int32, (sub, 1), 0)
    scores = []
    for i in range(1, nb):
        lo, hi = i * sub, (i + 1) * sub
        ref = bcum[lo - 1:lo, :]
        qt = q[lo:hi] * jnp.exp2(bcum[lo:hi] - ref)
        kt = k * jnp.exp2(jnp.where(row < lo, ref - bcum, _MASKED_EXPONENT))
        scores.append(_mm(_head_stack(qt, hk), kt, _NT))
    yield
    outs = [cross[0:sub]] + [cross[i * sub:(i + 1) * sub] + _head_unstack(_mm(scores[i - 1], vb), hv, sub)
                             for i in range(1, nb)]
    ps = []
    for i in range(nb):
        lo, hi = i * sub, (i + 1) * sub
        bi, qi, ki = bcum[lo:hi], q[lo:hi], k[lo:hi]
        for s in range(sub):
            bs = jnp.broadcast_to(bi[s:s + 1, :], (sub, w))
            ks = jnp.broadcast_to(ki[s:s + 1, :], (sub, w))
            e = jnp.exp2(jnp.where(trow >= s, bi - bs, _MASKED_EXPONENT))
            ps.append(qi * ks * e)
    rep = _mm(jnp.concatenate(ps, axis=0), hexp)
    yield
    for i in range(nb):
        vi = v[i * sub:(i + 1) * sub]
        acc = outs[i]
        for s in range(sub):
            r0 = (i * sub + s) * sub
            acc = acc + rep[r0:r0 + sub] * jnp.broadcast_to(vi[s:s + 1, :], (sub, BRANCH_WIDTH))
        outs[i] = acc
    return jnp.concatenate(outs, axis=0)


def _gla_body(x_ref, w_ref, w2_ref, b_ref, nw_ref, tri_ref, hexp_ref, bdt_ref, avg_ref,
              o_ref, z_scr, st_ref, *, bb, ts, c, sub):
    @pl.when(pl.program_id(1) == 0)
    def _():
        st_ref[...] = jnp.zeros_like(st_ref)

    z_scr[...] = jnp.dot(x_ref[...].reshape(bb * ts, x_ref.shape[2]), w_ref[...], preferred_element_type=F32)
    hk = _lane_heads(HEADS * GLA_DK, GLA_DK)
    hv = _lane_heads(BRANCH_WIDTH, HEAD_DV)

    def one(ci, bi):
        trows = pl.ds(pl.multiple_of(ci * c, c), c)
        rows = pl.ds(pl.multiple_of(ci * c + bi * ts, c), c)
        q = z_scr[rows, 0:128] * (GLA_DK ** -0.5)
        k = z_scr[rows, 128:256]
        v = z_scr[rows, 256:512]
        g = z_scr[rows, 512:768]
        gl = z_scr[rows, 768:896]
        glog = _log_sigmoid(_mm(gl, w2_ref[...]) + b_ref[...]) / GLA_GATE_TEMP
        yield
        o = yield from _gated_chunk(q, k, v, glog, st_ref.at[bi], tri_ref[...], hexp_ref[...], bdt_ref[...],
                                    hk, hv, c, sub)
        yield
        ms = _seg_mean(o * o, avg_ref[...])
        y = o * lax.rsqrt(ms + HEAD_EPS) * nw_ref[...]
        o_ref[bi, trows, :] = (y * _silu(g)).astype(o_ref.dtype)

    def chunk(ci, carry):
        _round_robin([one(ci, bi) for bi in range(bb)])
        return carry

    lax.fori_loop(0, ts // c, chunk, 0)


def _hgrn_body(x_ref, w_ref, lbl_ref, nw_ref, tri_ref, hexp_ref, bdt_ref, avg_ref,
               o_ref, z_scr, st_ref, *, bb, ts, c, sub, layer):
    @pl.when(pl.program_id(1) == 0)
    def _():
        st_ref[...] = jnp.zeros_like(st_ref)

    z_scr[...] = jnp.dot(x_ref[...].reshape(bb * ts, x_ref.shape[2]), w_ref[...], preferred_element_type=F32)
    hk = _lane_heads(HEADS * HGRN_DK, HGRN_DK)
    hv = _lane_heads(BRANCH_WIDTH, HEAD_DV)
    logits = lbl_ref[...]
    ex = jnp.exp(logits - jnp.max(logits, axis=0, keepdims=True))
    prob = ex / jnp.sum(ex, axis=0, keepdims=True)
    lb = jnp.zeros((1, HEADS * HGRN_DK), F32)
    for j in range(1, layer + 1):
        lb = lb + prob[j:j + 1, :]

    def one(ci, bi):
        trows = pl.ds(pl.multiple_of(ci * c, c), c)
        rows = pl.ds(pl.multiple_of(ci * c + bi * ts, c), c)
        q = z_scr[rows, 0:256]
        fz = z_scr[rows, 256:512]
        v = z_scr[rows, 512:768]
        g = z_scr[rows, 768:1024]
        f = lb + (1.0 - lb) * _sigmoid(fz)
        glog = jnp.log(f)
        k = (1.0 - lb) * _sigmoid(-fz)
        o = yield from _gated_chunk(q, k, v, glog, st_ref.at[bi], tri_ref[...], hexp_ref[...], bdt_ref[...],
                                    hk, hv, c, sub)
        yield
        ms = _seg_mean(o * o, avg_ref[...])
        y = o * lax.rsqrt(ms + HEAD_EPS) * nw_ref[...]
        o_ref[bi, trows, :] = (y * _silu(g)).astype(o_ref.dtype)

    def chunk(ci, carry):
        _round_robin([one(ci, bi) for bi in range(bb)])
        return carry

    lax.fori_loop(0, ts // c, chunk, 0)


def _gated_consts(wk, dk, c):
    tri = jnp.asarray(np.tril(np.ones((c, c), np.float32))).astype(BF16)
    k_head = np.arange(wk) // dk
    v_head = np.arange(BRANCH_WIDTH) // HEAD_DV
    same = (k_head[:, None] == v_head[None, :]).astype(np.float32)
    hexp = jnp.asarray(same).astype(BF16)
    bdt = jnp.asarray(same.T)
    return tri, hexp, bdt


def _gla_call(xb, w, w2p, bias, norm_w, avg):
    wk = HEADS * GLA_DK
    args = (w2p, bias, norm_w) + _gated_consts(wk, GLA_DK, GATED_CHUNK) + (avg,)
    body = functools.partial(_gla_body, c=GATED_CHUNK, sub=GATED_SUB)
    return _mixer_pallas(body, "gla", xb, w, args, [_const_spec(a.shape) for a in args], (BRANCH_WIDTH, wk),
                         SEQ_PER_STEP)


def _hgrn_call(xb, w, lb_logits, norm_w, avg, layer):
    wk = HEADS * HGRN_DK
    args = (lb_logits, norm_w) + _gated_consts(wk, HGRN_DK, GATED_CHUNK) + (avg,)
    body = functools.partial(_hgrn_body, c=GATED_CHUNK, sub=GATED_SUB, layer=layer)
    return _mixer_pallas(body, "hgrn2", xb, w, args, [_const_spec(a.shape) for a in args], (BRANCH_WIDTH, wk),
                         SEQ_PER_STEP)


def _rwkv_body(x_ref, w_ref, mu_ref, w0_ref, w2_ref, a0_ref, a2_ref, g2_ref, kk_ref, ka_ref, rk_ref,
               lnw_ref, lnb_ref, tri_ref, ms_ref, mi_ref, eye_ref, avg_ref,
               o_ref, z_scr, st_ref, last_ref, *, bb, ts, c):
    @pl.when(pl.program_id(1) == 0)
    def _():
        st_ref[...] = jnp.zeros_like(st_ref)
        last_ref[...] = jnp.zeros_like(last_ref)

    rowi = lax.broadcasted_iota(jnp.int32, (ts, 1), 0)
    for bi in range(bb):
        z = jnp.dot(x_ref[bi], w_ref[...], preferred_element_type=F32)
        prev = jnp.where(rowi == 0, jnp.broadcast_to(last_ref[bi, 7:8, :], z.shape), pltpu.roll(z, 1, 0))
        last_ref[bi] = z[ts - 8:ts, :]
        z_scr[bi * ts:(bi + 1) * ts, :] = z + (prev - z) * mu_ref[...]

    hl = _lane_heads(BRANCH_WIDTH, RWKV_N)
    mm = _mm

    def one(ci, bi):
        trows = pl.ds(pl.multiple_of(ci * c, c), c)
        rows = pl.ds(pl.multiple_of(ci * c + bi * ts, c), c)
        r = z_scr[rows, 0:256]
        k = z_scr[rows, 256:512]
        v = z_scr[rows, 512:768]
        wl = z_scr[rows, 768:832]
        al = z_scr[rows, 832:896]
        gl = z_scr[rows, 896:1024]
        lw = -RWKV_DECAY_SCALE * _sigmoid(w0_ref[...] + _mm(jnp.tanh(wl), w2_ref[...], na=2, nb=2))
        a = _sigmoid(a0_ref[...] + _mm(al, a2_ref[...]))
        g = _mm(_sigmoid(gl), g2_ref[...])
        kk = k * kk_ref[...]
        kk = kk * lax.rsqrt(_mm(kk * kk, avg_ref[...]) * float(RWKV_N) + 1e-12)
        k2 = k * (1.0 + (a - 1.0) * ka_ref[...])
        beta = kk * a
        cw = _mm(tri_ref[...], lw, na=1, nb=2)
        e_in = jnp.exp(cw)
        e_out = jnp.exp(-cw)
        at = -kk * jnp.exp(cw - lw)
        rt = r * e_in
        bt = beta * e_out
        kt = k2 * e_out
        n = HEADS * c
        hs_a = _head_stack(at, hl).astype(BF16)
        hs_r = _head_stack(rt, hl)
        lhs_ar = jnp.concatenate([hs_a, hs_r.astype(BF16)], axis=0)
        rhs_bk = jnp.concatenate([bt] * HEADS + [kt] * HEADS, axis=0).astype(BF16)
        yield
        gram = mm(lhs_ar, rhs_bk, _NT)
        a_ab = gram[0:n, 0:n] * ms_ref[...]
        a_ak = (gram[0:n, n:2 * n] * ms_ref[...]).astype(BF16)
        a_rb = (gram[n:2 * n, 0:n] * mi_ref[...]).astype(BF16)
        a_rk = (gram[n:2 * n, n:2 * n] * mi_ref[...]).astype(BF16)
        tinv = eye_ref[...] + a_ab
        pw = a_ab.astype(BF16)
        yield
        pw = mm(pw, pw).astype(BF16)
        v_st = _head_stack(v, hl).astype(BF16)
        akrk_v = mm(jnp.concatenate([a_ak, a_rk], axis=0), v_st)
        av = akrk_v[0:n].astype(BF16)
        for _ in range(int(math.log2(c)) - 2):
            yield
            both = mm(jnp.concatenate([pw, tinv.astype(BF16)], axis=0), pw)
            tinv = tinv + both[n:2 * n]
            pw = both[0:n].astype(BF16)
        yield
        tinv = (tinv + mm(tinv, pw)).astype(BF16)
        yield
        w12 = mm(tinv, jnp.concatenate([av, hs_a], axis=1)).astype(BF16)
        yield
        aw = mm(a_rb, w12)
        cwl = cw[c - 1:c, :]
        dec = jnp.exp(cwl - cw)
        hs_bh = _head_stack(beta * dec, hl).astype(BF16)
        hs_kh = _head_stack(k2 * dec, hl).astype(BF16)
        bw = mm(hs_bh, w12, _TN)
        yield
        st = st_ref[bi].astype(BF16)
        wv = BRANCH_WIDTH
        y_st = mm(hs_r + aw[:, wv:2 * wv], st) + aw[:, 0:wv] + akrk_v[n:2 * n]
        y = y_st[0:c] + y_st[c:2 * c] + y_st[2 * c:3 * c] + y_st[3 * c:4 * c]
        m_t = eye_ref[...] * jnp.exp(cwl) + bw[:, wv:2 * wv]
        st_ref[bi] = mm(m_t, st) + bw[:, 0:wv] + mm(hs_kh, v_st, _TN)
        yield
        mean = _seg_mean(y, avg_ref[...])
        d = y - mean
        var = _seg_mean(d * d, avg_ref[...])
        yn = d * lax.rsqrt(var + RWKV_GN_EPS) * lnw_ref[...] + lnb_ref[...]
        bonus = _mm(r * k2 * rk_ref[...], avg_ref[...]) * float(RWKV_N) * v
        o_ref[bi, trows, :] = ((yn + bonus) * g).astype(o_ref.dtype)

    def chunk(ci, carry):
        _round_robin([one(ci, bi) for bi in range(bb)])
        return carry

    lax.fori_loop(0, ts // c, chunk, 0, unroll=2)


def _rwkv_call(xb, w, mu, w0, w2, a0, a2, g2, k_k, k_a, r_k, ln_w, ln_b, avg):
    b, s, d = xb.shape
    ts, c = TIME_TILE, RWKV_CHUNK
    n = HEADS * c
    tri = jnp.asarray(np.tril(np.ones((c, c), np.float32))).astype(BF16)
    ridx = np.arange(n)
    same = (ridx[:, None] // c) == (ridx[None, :] // c)
    ms = jnp.asarray((same & ((ridx[None, :] % c) < (ridx[:, None] % c))).astype(np.float32))
    mi = jnp.asarray((same & ((ridx[None, :] % c) <= (ridx[:, None] % c))).astype(np.float32))
    eye = jnp.asarray(np.eye(n, dtype=np.float32))
    row = lambda a: a.reshape(1, -1).astype(F32)
    args = (row(mu), row(w0), w2.astype(F32), row(a0), a2.astype(F32), g2.astype(F32),
            row(k_k), row(k_a), row(r_k), row(ln_w), row(ln_b), tri, ms, mi, eye, avg)
    body = functools.partial(_rwkv_body, c=c)
    return _mixer_pallas(body, "rwkv7", xb, w, args, [_const_spec(a.shape) for a in args], (n, BRANCH_WIDTH),
                         RWKV_SEQ_PER_STEP, extra_scratch=[pltpu.VMEM((RWKV_SEQ_PER_STEP, 8, w.shape[1]), F32)])


def _merge_body(x32_ref, xb_ref, b0_ref, b1_ref, b2_ref, b3_ref, wg_ref, wb_ref, wm_ref, lw_ref, lb_ref,
                o32_ref, o16_ref):
    tm = x32_ref.shape[0]
    for r0 in range(0, tm, DENSE_SUB_ROWS):
        rows = slice(r0, r0 + DENSE_SUB_ROWS)
        xb = xb_ref[rows, :]
        acc = None
        for n, br_ref in enumerate((b0_ref, b1_ref, b2_ref, b3_ref)):
            gate = jnp.dot(xb, wg_ref[:, n * D_MODEL:(n + 1) * D_MODEL], preferred_element_type=F32)
            br = jnp.dot(br_ref[rows, :], wb_ref[n], preferred_element_type=F32)
            t = _sigmoid(gate) * br
            acc = t if acc is None else acc + t
        y = jnp.dot(acc.astype(BF16), wm_ref[...], preferred_element_type=F32)
        out = _layer_norm_rows(ALPHA * x32_ref[rows, :] + y, lw_ref[...], lb_ref[...])
        o32_ref[rows, :] = out
        o16_ref[rows, :] = out.astype(BF16)


def _merge_call(x32, xb, branches, wg, wb, wm, ln_w, ln_b):
    t, d = x32.shape
    tm = ROW_TILE
    row_spec = lambda width: pl.BlockSpec((tm, width), lambda i: (i, 0))
    full = lambda a: pl.BlockSpec(a.shape, lambda i: (0,) * a.ndim)
    lw = ln_w.reshape(1, d)
    lb = ln_b.reshape(1, d)
    return pl.pallas_call(
        _merge_body,
        grid=(t // tm,),
        in_specs=[row_spec(d), row_spec(d)] + [row_spec(BRANCH_WIDTH)] * N_BRANCH
                 + [full(wg), full(wb), full(wm), full(lw), full(lb)],
        out_specs=[row_spec(d), row_spec(d)],
        out_shape=[jax.ShapeDtypeStruct((t, d), F32), jax.ShapeDtypeStruct((t, d), BF16)],
        compiler_params=pltpu.CompilerParams(dimension_semantics=("parallel",),
                                             vmem_limit_bytes=V7X_VMEM_LIMIT_BYTES),
        name="merge",
    )(x32, xb, *branches, wg, wb, wm, lw, lb)


FFN_CHUNK = 1408


def _ffn_body(x32_ref, xb_ref, p_ref, wup_ref, cw_ref, wdn_ref, wpg_ref, wpp_ref, lw_ref, lb_ref,
              o32_ref, o16_ref, carry_ref, *, tm):
    @pl.when(pl.program_id(1) == 0)
    def _():
        carry_ref[...] = jnp.zeros_like(carry_ref)

    sub = DENSE_SUB_ROWS
    rowi = lax.broadcasted_iota(jnp.int32, (sub, 1), 0)
    chunks = [(ci * FFN_CHUNK, (ci + 1) * FFN_CHUNK) for ci in range(D_FF // FFN_CHUNK)]
    prev_u = [carry_ref[:, lo:hi] for lo, hi in chunks]
    for r0 in range(0, tm, sub):
        rows = slice(r0, r0 + sub)
        xb = xb_ref[0, rows, :]
        ups = [(jnp.dot(xb, wup_ref[:, lo:hi], preferred_element_type=F32),
                jnp.dot(xb, wup_ref[:, D_FF + lo:D_FF + hi], preferred_element_type=F32)) for lo, hi in chunks]
        hidden = []
        for ci, ((lo, hi), (u, v)) in enumerate(zip(chunks, ups)):
            p1 = jnp.broadcast_to(prev_u[ci][7:8, :], u.shape)
            p2 = jnp.broadcast_to(prev_u[ci][6:7, :], u.shape)
            u1 = jnp.where(rowi == 0, p1, pltpu.roll(u, 1, 0))
            u2 = jnp.where(rowi == 0, p2, jnp.where(rowi == 1, p1, pltpu.roll(u, 2, 0)))
            prev_u[ci] = u[sub - 8:sub, :]
            uc = cw_ref[0:1, lo:hi] * u2 + cw_ref[1:2, lo:hi] * u1 + cw_ref[2:3, lo:hi] * u
            hidden.append((_gelu_tanh(uc) * v).astype(BF16))
        acc = jnp.dot(hidden[0], wdn_ref[chunks[0][0]:chunks[0][1], :], preferred_element_type=F32)
        ple = _sigmoid(jnp.dot(xb, wpg_ref[...], preferred_element_type=F32)) * jnp.dot(
            p_ref[0, rows, :].astype(BF16), wpp_ref[...], preferred_element_type=F32)
        for (lo, hi), h in zip(chunks[1:], hidden[1:]):
            acc = acc + jnp.dot(h, wdn_ref[lo:hi, :], preferred_element_type=F32)
        out = _layer_norm_rows(ALPHA * x32_ref[0, rows, :] + acc + ple, lw_ref[...], lb_ref[...])
        o32_ref[0, rows, :] = out
        o16_ref[0, rows, :] = out.astype(BF16)
    for (lo, hi), u_tail in zip(chunks, prev_u):
        carry_ref[:, lo:hi] = u_tail


def _ffn_call(x32, xb, p_all, layer, wup, convw, wdn, wpg, wpp, ln_w, ln_b):
    b, s, d = x32.shape
    tm = ROW_TILE
    row_spec = lambda width: pl.BlockSpec((1, tm, width), lambda i, t: (i, t, 0))
    p_spec = pl.BlockSpec((None, 1, tm, PLE_DIM), lambda i, t: (layer, i, t, 0))
    full = lambda a: pl.BlockSpec(a.shape, lambda i, t: (0,) * a.ndim, pipeline_mode=pl.Buffered(1))
    lw = ln_w.reshape(1, d)
    lb = ln_b.reshape(1, d)
    body = functools.partial(_ffn_body, tm=tm)
    return pl.pallas_call(
        body,
        grid=(b, s // tm),
        in_specs=[row_spec(d), row_spec(d), p_spec]
                 + [full(wup), full(convw), full(wdn), full(wpg), full(wpp), full(lw), full(lb)],
        out_specs=[row_spec(d), row_spec(d)],
        out_shape=[jax.ShapeDtypeStruct((b, s, d), F32), jax.ShapeDtypeStruct((b, s, d), BF16)],
        scratch_shapes=[pltpu.VMEM((8, D_FF), F32)],
        compiler_params=pltpu.CompilerParams(dimension_semantics=("parallel", "arbitrary"),
                                             vmem_limit_bytes=V7X_VMEM_LIMIT_BYTES),
        name="conv_ffn",
    )(x32, xb, p_all, wup, convw, wdn, wpg, wpp, lw, lb)


def _group_weights(w_in_l):
    offs = np.cumsum((0,) + GROUP_COLS)
    w_ret = w_in_l[:, offs[0]:offs[1]]
    w_rwkv = w_in_l[:, offs[1]:offs[2]]
    w_gla = w_in_l[:, offs[2]:offs[3]]
    w_hgrn = w_in_l[:, offs[3]:offs[4]]
    w_gate = w_in_l[:, offs[4]:offs[5]]
    perm = _ret_perm()
    nq = HEADS * RET_DK
    w_ret = jnp.concatenate([w_ret[:, perm], w_ret[:, nq + perm], w_ret[:, 2 * nq:]], axis=1)
    gq = HEADS * GLA_DK
    o_v, o_gl, o_g = 2 * gq, 2 * gq + BRANCH_WIDTH, 2 * gq + BRANCH_WIDTH + GLA_GATE_LORA
    pad = jnp.zeros((w_gla.shape[0], LANES - GLA_GATE_LORA), w_gla.dtype)
    w_gla = jnp.concatenate([w_gla[:, :o_gl], w_gla[:, o_g:], w_gla[:, o_gl:o_g], pad], axis=1)
    cast = lambda a: a.astype(BF16)
    return cast(w_ret), cast(w_rwkv), cast(w_gla), cast(w_hgrn), cast(w_gate)


def kernel(x, p, ln_in_w, ln_in_b, w_in, rwkv_mu, rwkv_w0, rwkv_w2, rwkv_a0, rwkv_a2, rwkv_g2, rwkv_k_k, rwkv_k_a, rwkv_r_k, rwkv_ln_w, rwkv_ln_b, gla_w2, gla_b, gla_norm_w, hgrn_lb_logits, hgrn_norm_w, w_branch, w_mix_out, ln_mix_w, ln_mix_b, w_ffn_up, ffn_conv, w_ffn_down, w_ple_gate, w_ple_proj, ln_ffn_w, ln_ffn_b):
    b, s, d = x.shape
    depth = w_in.shape[0]
    t = b * s
    v_head = np.arange(BRANCH_WIDTH) // HEAD_DV
    avg = jnp.asarray((v_head[:, None] == v_head[None, :]).astype(np.float32) / HEAD_DV).astype(BF16)
    rope = _rope_tables(s)

    x32, xb = _ln_call(x.reshape(t, d), ln_in_w, ln_in_b)
    for i in range(depth):
        w_ret, w_rwkv, w_gla, w_hgrn, w_gate = _group_weights(w_in[i])
        xb3 = xb.reshape(b, s, d)
        o_ret = _ret_call(xb3, w_ret, rope, avg)
        o_rwkv = _rwkv_call(xb3, w_rwkv, rwkv_mu[i], rwkv_w0[i], rwkv_w2[i], rwkv_a0[i], rwkv_a2[i],
                            rwkv_g2[i], rwkv_k_k[i], rwkv_k_a[i], rwkv_r_k[i], rwkv_ln_w[i], rwkv_ln_b[i], avg)
        w2p = jnp.concatenate([gla_w2[i], jnp.zeros((LANES - GLA_GATE_LORA, gla_w2.shape[2]), F32)], axis=0)
        o_gla = _gla_call(xb3, w_gla, w2p, gla_b[i].reshape(1, -1), jnp.tile(gla_norm_w[i], HEADS).reshape(1, -1), avg)
        o_hgrn = _hgrn_call(xb3, w_hgrn, hgrn_lb_logits.astype(F32), jnp.tile(hgrn_norm_w[i], HEADS).reshape(1, -1),
                            avg, i)
        branches = [o.reshape(t, BRANCH_WIDTH) for o in (o_ret, o_rwkv, o_gla, o_hgrn)]
        x32, xb = _merge_call(x32, xb, branches, w_gate, w_branch[i].astype(BF16), w_mix_out[i].astype(BF16),
                              ln_mix_w[i], ln_mix_b[i])
        x32, xb = _ffn_call(x32.reshape(b, s, d), xb.reshape(b, s, d), p, i, w_ffn_up[i].astype(BF16), ffn_conv[i],
                            w_ffn_down[i].astype(BF16), w_ple_gate[i].astype(BF16), w_ple_proj[i].astype(BF16),
                            ln_ffn_w[i], ln_ffn_b[i])
        x32 = x32.reshape(t, d)
        xb = xb.reshape(t, d)
    return x32.reshape(b, s, d)
```

```python
import functools
import math

import numpy as np
import jax
import jax.numpy as jnp
from jax import lax
from jax.experimental import pallas as pl
from jax.experimental.pallas import tpu as pltpu

F32 = jnp.float32
BF16 = jnp.bfloat16

D_MODEL = 1024
DEPTH = 2
N_BRANCH = 4
HEADS = 4
BRANCH_WIDTH = D_MODEL // 4
HEAD_DV = BRANCH_WIDTH // HEADS
RET_DK = HEAD_DV // 2
GLA_DK = HEAD_DV // 2
HGRN_DK = HEAD_DV
RWKV_N = HEAD_DV
RWKV_W_LORA = 64
RWKV_A_LORA = 64
RWKV_G_LORA = 128
GLA_GATE_LORA = 16
GLA_GATE_TEMP = 16.0
RET_ROPE_BASE = 10000.0
RWKV_DECAY_SCALE = 0.6065306597126334
D_FF = 2816
PLE_DIM = 256
LN_EPS = 1e-5
HEAD_EPS = 1e-6
RWKV_GN_EPS = 64e-5
ALPHA = (2.0 * DEPTH) ** 0.25

RET_COLS = (HEADS * RET_DK, HEADS * RET_DK, BRANCH_WIDTH, BRANCH_WIDTH)
RWKV_COLS = (BRANCH_WIDTH, BRANCH_WIDTH, BRANCH_WIDTH, RWKV_W_LORA, RWKV_A_LORA, RWKV_G_LORA)
GLA_COLS = (HEADS * GLA_DK, HEADS * GLA_DK, BRANCH_WIDTH, GLA_GATE_LORA, BRANCH_WIDTH)
HGRN_COLS = (HEADS * HGRN_DK, HEADS * HGRN_DK, BRANCH_WIDTH, BRANCH_WIDTH)
GROUP_COLS = (sum(RET_COLS), sum(RWKV_COLS), sum(GLA_COLS), sum(HGRN_COLS), N_BRANCH * D_MODEL)

V7X_VMEM_LIMIT_BYTES = 56 * 1024 * 1024
LANES = 128

TIME_TILE = 512
SEQ_PER_STEP = 4
RWKV_SEQ_PER_STEP = 4
RET_CHUNK = 128
GATED_CHUNK = 64
GATED_SUB = 8
RWKV_CHUNK = 64
ROW_TILE = 512
DENSE_SUB_ROWS = 256


_NN = (((1,), (0,)), ((), ()))
_NT = (((1,), (1,)), ((), ()))
_TN = (((0,), (0,)), ((), ()))


def _split(x, n):
    if x.dtype == BF16:
        return [x]
    parts = []
    r = x
    for i in range(n):
        p = r.astype(BF16)
        parts.append(p)
        if i + 1 < n:
            r = r - p.astype(F32)
    return parts


def _mm(a, b, dims=_NN, na=1, nb=1):
    ap = _split(a, na)
    bp = _split(b, nb)
    order = max(len(ap), len(bp))
    acc = None
    for i, x in enumerate(ap):
        for j, y in enumerate(bp):
            if i + j >= order:
                continue
            t = lax.dot_general(x, y, dims, preferred_element_type=F32)
            acc = t if acc is None else acc + t
    return acc


def _sigmoid(x):
    return 1.0 / (1.0 + jnp.exp(-x))


def _silu(x):
    return x * _sigmoid(x)


def _log_sigmoid(x):
    return jnp.minimum(x, 0.0) - jnp.log1p(jnp.exp(-jnp.abs(x)))


def _gelu_tanh(x):
    return 0.5 * x * (1.0 + jnp.tanh(math.sqrt(2.0 / math.pi) * (x + 0.044715 * (x * x * x))))


def _layer_norm_rows(h, w, b):
    mu = jnp.mean(h, axis=-1, keepdims=True)
    hc = h - mu
    var = jnp.mean(hc * hc, axis=-1, keepdims=True)
    return hc * lax.rsqrt(var + LN_EPS) * w + b


def _head_stack(x, head_of_lane):
    return jnp.concatenate([jnp.where(head_of_lane == h, x, 0.0) for h in range(HEADS)], axis=0)


def _head_unstack(y, head_of_lane, c):
    acc = None
    for h in range(HEADS):
        t = jnp.where(head_of_lane == h, y[h * c:(h + 1) * c], 0.0)
        acc = t if acc is None else acc + t
    return acc


def _seg_mean(x, avg):
    return _mm(x, avg)


def _round_robin(stage_generators):
    live = list(stage_generators)
    while live:
        live = [g for g in live if next(g, _DONE) is not _DONE]


_DONE = object()


def _lane_heads(width, per_head):
    return lax.broadcasted_iota(jnp.int32, (1, width), 1) // per_head


def _ln_body(x_ref, w_ref, b_ref, o32_ref, o16_ref):
    y = _layer_norm_rows(x_ref[...], w_ref[...], b_ref[...])
    o32_ref[...] = y
    o16_ref[...] = y.astype(BF16)


def _ln_call(x2, w, b):
    t, d = x2.shape
    return pl.pallas_call(
        _ln_body,
        grid=(t // ROW_TILE,),
        in_specs=[
            pl.BlockSpec((ROW_TILE, d), lambda i: (i, 0)),
            pl.BlockSpec((1, d), lambda i: (0, 0)),
            pl.BlockSpec((1, d), lambda i: (0, 0)),
        ],
        out_specs=[
            pl.BlockSpec((ROW_TILE, d), lambda i: (i, 0)),
            pl.BlockSpec((ROW_TILE, d), lambda i: (i, 0)),
        ],
        out_shape=[jax.ShapeDtypeStruct((t, d), F32), jax.ShapeDtypeStruct((t, d), BF16)],
        compiler_params=pltpu.CompilerParams(dimension_semantics=("parallel",)),
        name="ln_in",
    )(x2, w.reshape(1, d), b.reshape(1, d))


def _ret_body(x_ref, w_ref, cos_ref, sin_ref, dm_ref, xi_ref, zeta_ref, gc_ref, bd_ref, avg_ref,
              o_ref, z_scr, st_ref, *, bb, ts, c):
    @pl.when(pl.program_id(1) == 0)
    def _():
        st_ref[...] = jnp.zeros_like(st_ref)

    z_scr[...] = jnp.dot(x_ref[...].reshape(bb * ts, x_ref.shape[2]), w_ref[...], preferred_element_type=F32)
    hq = (lax.broadcasted_iota(jnp.int32, (1, 128), 1) % 64) // (RET_DK // 2)
    hv = _lane_heads(BRANCH_WIDTH, HEAD_DV)

    def one(ci, bi):
        trows = pl.ds(pl.multiple_of(ci * c, c), c)
        rows = pl.ds(pl.multiple_of(ci * c + bi * ts, c), c)
        zq = z_scr[rows, 0:128]
        zk = z_scr[rows, 128:256]
        v = z_scr[rows, 256:512]
        g = z_scr[rows, 512:768]
        cs = cos_ref[trows, :]
        sn = sin_ref[trows, :]
        q = zq * cs + pltpu.roll(zq, 64, 1) * sn
        k = (zk * cs + pltpu.roll(zk, 64, 1) * sn) * (RET_DK ** -0.5)
        yield
        sc = _mm(_head_stack(q, hq), k, _NT)
        r_state = st_ref[bi]
        cross = _mm(q * xi_ref[...], r_state)
        st_ref[bi] = r_state * gc_ref[...] + _mm(k, v * zeta_ref[...], _TN) * bd_ref[...]
        p = sc * dm_ref[...]
        yield
        inner = _head_unstack(_mm(p, v), hv, c)
        o = inner + cross
        yield
        mean = _seg_mean(o, avg_ref[...])
        d = o - mean
        yield
        var = _seg_mean(d * d, avg_ref[...])
        y = d * lax.rsqrt(var + HEAD_EPS)
        o_ref[bi, trows, :] = (_silu(g) * y).astype(o_ref.dtype)

    def chunk(ci, carry):
        _round_robin([one(ci, bi) for bi in range(bb)])
        return carry

    lax.fori_loop(0, ts // c, chunk, 0, unroll=2)


def _const_spec(shape):
    nd = len(shape)
    return pl.BlockSpec(shape, lambda b, t: (0,) * nd)


def _ret_tables(c):
    hs = np.arange(HEADS, dtype=np.float32)
    log_gamma = jnp.log1p(-jnp.exp2(-5.0 - jnp.asarray(hs)))
    pos = jnp.arange(c, dtype=F32)
    causal = pos[:, None] >= pos[None, :]
    rel = jnp.where(causal, pos[:, None] - pos[None, :], 0.0)
    decay = jnp.where(causal[None], jnp.exp(log_gamma[:, None, None] * rel[None]), 0.0)
    dm = decay.reshape(HEADS * c, c)
    q_head = (np.arange(128) % 64) // (RET_DK // 2)
    v_head = np.arange(BRANCH_WIDTH) // HEAD_DV
    xi = jnp.exp(log_gamma[None, :] * (pos[:, None] + 1.0))[:, q_head]
    zeta = jnp.exp(log_gamma[None, :] * (c - 1.0 - pos[:, None]))[:, v_head]
    gc = jnp.exp(log_gamma * c)[v_head][None, :]
    bd = jnp.asarray((q_head[:, None] == v_head[None, :]).astype(np.float32))
    return dm, xi, zeta, gc, bd


def _rope_tables(s):
    half = RET_DK // 2
    theta = 1.0 / (RET_ROPE_BASE ** jnp.linspace(0.0, 1.0, half))
    ang = jnp.arange(s, dtype=F32)[:, None] * theta[None, :]
    idx = np.arange(128) % half
    cos = jnp.cos(ang)[:, idx]
    sin = jnp.sin(ang)[:, idx]
    sign = jnp.asarray(np.where(np.arange(128) < 64, -1.0, 1.0).astype(np.float32))
    return cos, sin * sign[None, :]


def _ret_perm():
    half = RET_DK // 2
    l = np.arange(128)
    h = (l % 64) // half
    i = l % half
    return h * RET_DK + 2 * i + (l >= 64)


def _mixer_pallas(body, name, xb, w, args, arg_specs, state_shape, bb, extra_scratch=()):
    b, s, d = xb.shape
    ts = TIME_TILE
    return pl.pallas_call(
        functools.partial(body, bb=bb, ts=ts),
        grid=(b // bb, s // ts),
        in_specs=[pl.BlockSpec((bb, ts, d), lambda i, t: (i, t, 0)), _const_spec(w.shape)] + list(arg_specs),
        out_specs=pl.BlockSpec((bb, ts, BRANCH_WIDTH), lambda i, t: (i, t, 0)),
        out_shape=jax.ShapeDtypeStruct((b, s, BRANCH_WIDTH), BF16),
        scratch_shapes=[pltpu.VMEM((bb * ts, w.shape[1]), F32), pltpu.VMEM((bb,) + tuple(state_shape), F32)]
                       + list(extra_scratch),
        compiler_params=pltpu.CompilerParams(dimension_semantics=("parallel", "arbitrary"),
                                             vmem_limit_bytes=V7X_VMEM_LIMIT_BYTES),
        name=name,
    )(xb, w, *args)


def _ret_call(xb, w, s_tables, avg):
    ts, c = TIME_TILE, RET_CHUNK
    cos, sin = s_tables
    consts = _ret_tables(c) + (avg,)
    body = functools.partial(_ret_body, c=c)
    time_spec = pl.BlockSpec((ts, cos.shape[1]), lambda i, t: (t, 0))
    return _mixer_pallas(body, "retention", xb, w, (cos, sin) + consts,
                         [time_spec, time_spec] + [_const_spec(a.shape) for a in consts],
                         (HEADS * RET_DK, BRANCH_WIDTH), SEQ_PER_STEP)


_MASKED_EXPONENT = -1e30
_LOG2E = math.log2(math.e)


def _gated_chunk(q, k, v, glog, st_ref, tri, hexp, bdt, hk, hv, c, sub):
    w = q.shape[1]
    nb = c // sub
    vb = v.astype(BF16)
    bcum = _mm(tri, glog, na=1, nb=2) * _LOG2E
    yield
    st = st_ref[...]
    cross = _mm(q * jnp.exp2(bcum), st, _NT)
    blast = bcum[c - 1:c, :]
    kd = k * jnp.exp2(blast - bcum)
    st_ref[...] = st * jnp.exp2(blast) + _mm(v, kd, _TN) * bdt
    row = lax.broadcasted_iota(jnp.int32, (c, 1), 0)
    trow = lax.broadcasted_iota(jnp.int32, (sub, 1), 0)
    scores = []
    for i in range(1, nb):
        lo, hi = i * sub, (i + 1) * sub
        ref = bcum[lo - 1:lo, :]
        qt = q[lo:hi] * jnp.exp2(bcum[lo:hi] - ref)
        kt = k * jnp.exp2(jnp.where(row < lo, ref - bcum, _MASKED_EXPONENT))
        scores.append(_mm(_head_stack(qt, hk), kt, _NT))
    yield
    outs = [cross[0:sub]] + [cross[i * sub:(i + 1) * sub] + _head_unstack(_mm(scores[i - 1], vb), hv, sub)
                             for i in range(1, nb)]
    ps = []
    for i in range(nb):
        lo, hi = i * sub, (i + 1) * sub
        bi, qi, ki = bcum[lo:hi], q[lo:hi], k[lo:hi]
        for s in range(sub):
            bs = jnp.broadcast_to(bi[s:s + 1, :], (sub, w))
            ks = jnp.broadcast_to(ki[s:s + 1, :], (sub, w))
            e = jnp.exp2(jnp.where(trow >= s, bi - bs, _MASKED_EXPONENT))
            ps.append(qi * ks * e)
    rep = _mm(jnp.concatenate(ps, axis=0), hexp)
    yield
    for i in range(nb):
        vi = v[i * sub:(i + 1) * sub]
        acc = outs[i]
        for s in range(sub):
            r0 = (i * sub + s) * sub
            acc = acc + rep[r0:r0 + sub] * jnp.broadcast_to(vi[s:s + 1, :], (sub, BRANCH_WIDTH))
        outs[i] = acc
    return jnp.concatenate(outs, axis=0)


def _gla_body(x_ref, w_ref, w2_ref, b_ref, nw_ref, tri_ref, hexp_ref, bdt_ref, avg_ref,
              o_ref, z_scr, st_ref, *, bb, ts, c, sub):
    @pl.when(pl.program_id(1) == 0)
    def _():
        st_ref[...] = jnp.zeros_like(st_ref)

    z_scr[...] = jnp.dot(x_ref[...].reshape(bb * ts, x_ref.shape[2]), w_ref[...], preferred_element_type=F32)
    hk = _lane_heads(HEADS * GLA_DK, GLA_DK)
    hv = _lane_heads(BRANCH_WIDTH, HEAD_DV)

    def one(ci, bi):
        trows = pl.ds(pl.multiple_of(ci * c, c), c)
        rows = pl.ds(pl.multiple_of(ci * c + bi * ts, c), c)
        q = z_scr[rows, 0:128] * (GLA_DK ** -0.5)
        k = z_scr[rows, 128:256]
        v = z_scr[rows, 256:512]
        g = z_scr[rows, 512:768]
        gl = z_scr[rows, 768:896]
        glog = _log_sigmoid(_mm(gl, w2_ref[...]) + b_ref[...]) / GLA_GATE_TEMP
        yield
        o = yield from _gated_chunk(q, k, v, glog, st_ref.at[bi], tri_ref[...], hexp_ref[...], bdt_ref[...],
                                    hk, hv, c, sub)
        yield
        ms = _seg_mean(o * o, avg_ref[...])
        y = o * lax.rsqrt(ms + HEAD_EPS) * nw_ref[...]
        o_ref[bi, trows, :] = (y * _silu(g)).astype(o_ref.dtype)

    def chunk(ci, carry):
        _round_robin([one(ci, bi) for bi in range(bb)])
        return carry

    lax.fori_loop(0, ts // c, chunk, 0, unroll=2)


def _hgrn_body(x_ref, w_ref, lbl_ref, nw_ref, tri_ref, hexp_ref, bdt_ref, avg_ref,
               o_ref, z_scr, st_ref, *, bb, ts, c, sub, layer):
    @pl.when(pl.program_id(1) == 0)
    def _():
        st_ref[...] = jnp.zeros_like(st_ref)

    z_scr[...] = jnp.dot(x_ref[...].reshape(bb * ts, x_ref.shape[2]), w_ref[...], preferred_element_type=F32)
    hk = _lane_heads(HEADS * HGRN_DK, HGRN_DK)
    hv = _lane_heads(BRANCH_WIDTH, HEAD_DV)
    logits = lbl_ref[...]
    ex = jnp.exp(logits - jnp.max(logits, axis=0, keepdims=True))
    prob = ex / jnp.sum(ex, axis=0, keepdims=True)
    lb = jnp.zeros((1, HEADS * HGRN_DK), F32)
    for j in range(1, layer + 1):
        lb = lb + prob[j:j + 1, :]

    def one(ci, bi):
        trows = pl.ds(pl.multiple_of(ci * c, c), c)
        rows = pl.ds(pl.multiple_of(ci * c + bi * ts, c), c)
        q = z_scr[rows, 0:256]
        fz = z_scr[rows, 256:512]
        v = z_scr[rows, 512:768]
        g = z_scr[rows, 768:1024]
        f = lb + (1.0 - lb) * _sigmoid(fz)
        glog = jnp.log(f)
        k = (1.0 - lb) * _sigmoid(-fz)
        o = yield from _gated_chunk(q, k, v, glog, st_ref.at[bi], tri_ref[...], hexp_ref[...], bdt_ref[...],
                                    hk, hv, c, sub)
        yield
        ms = _seg_mean(o * o, avg_ref[...])
        y = o * lax.rsqrt(ms + HEAD_EPS) * nw_ref[...]
        o_ref[bi, trows, :] = (y * _silu(g)).astype(o_ref.dtype)

    def chunk(ci, carry):
        _round_robin([one(ci, bi) for bi in range(bb)])
        return carry

    lax.fori_loop(0, ts // c, chunk, 0, unroll=2)


def _gated_consts(wk, dk, c):
    tri = jnp.asarray(np.tril(np.ones((c, c), np.float32))).astype(BF16)
    k_head = np.arange(wk) // dk
    v_head = np.arange(BRANCH_WIDTH) // HEAD_DV
    same = (k_head[:, None] == v_head[None, :]).astype(np.float32)
    hexp = jnp.asarray(same).astype(BF16)
    bdt = jnp.asarray(same.T)
    return tri, hexp, bdt


def _gla_call(xb, w, w2p, bias, norm_w, avg):
    wk = HEADS * GLA_DK
    args = (w2p, bias, norm_w) + _gated_consts(wk, GLA_DK, GATED_CHUNK) + (avg,)
    body = functools.partial(_gla_body, c=GATED_CHUNK, sub=GATED_SUB)
    return _mixer_pallas(body, "gla", xb, w, args, [_const_spec(a.shape) for a in args], (BRANCH_WIDTH, wk),
                         SEQ_PER_STEP)


def _hgrn_call(xb, w, lb_logits, norm_w, avg, layer):
    wk = HEADS * HGRN_DK
    args = (lb_logits, norm_w) + _gated_consts(wk, HGRN_DK, GATED_CHUNK) + (avg,)
    body = functools.partial(_hgrn_body, c=GATED_CHUNK, sub=GATED_SUB, layer=layer)
    return _mixer_pallas(body, "hgrn2", xb, w, args, [_const_spec(a.shape) for a in args], (BRANCH_WIDTH, wk),
                         SEQ_PER_STEP)


def _rwkv_body(x_ref, w_ref, mu_ref, w0_ref, w2_ref, a0_ref, a2_ref, g2_ref, kk_ref, ka_ref, rk_ref,
               lnw_ref, lnb_ref, tri_ref, ms_ref, mi_ref, eye_ref, avg_ref,
               o_ref, z_scr, st_ref, last_ref, *, bb, ts, c):
    @pl.when(pl.program_id(1) == 0)
    def _():
        st_ref[...] = jnp.zeros_like(st_ref)
        last_ref[...] = jnp.zeros_like(last_ref)

    rowi = lax.broadcasted_iota(jnp.int32, (ts, 1), 0)
    for bi in range(bb):
        z = jnp.dot(x_ref[bi], w_ref[...], preferred_element_type=F32)
        prev = jnp.where(rowi == 0, jnp.broadcast_to(last_ref[bi, 7:8, :], z.shape), pltpu.roll(z, 1, 0))
        last_ref[bi] = z[ts - 8:ts, :]
        z_scr[bi * ts:(bi + 1) * ts, :] = z + (prev - z) * mu_ref[...]

    hl = _lane_heads(BRANCH_WIDTH, RWKV_N)
    mm = _mm

    def one(ci, bi):
        trows = pl.ds(pl.multiple_of(ci * c, c), c)
        rows = pl.ds(pl.multiple_of(ci * c + bi * ts, c), c)
        r = z_scr[rows, 0:256]
        k = z_scr[rows, 256:512]
        v = z_scr[rows, 512:768]
        wl = z_scr[rows, 768:832]
        al = z_scr[rows, 832:896]
        gl = z_scr[rows, 896:1024]
        lw = -RWKV_DECAY_SCALE * _sigmoid(w0_ref[...] + _mm(jnp.tanh(wl), w2_ref[...], na=2, nb=2))
        a = _sigmoid(a0_ref[...] + _mm(al, a2_ref[...]))
        g = _mm(_sigmoid(gl), g2_ref[...])
        kk = k * kk_ref[...]
        kk = kk * lax.rsqrt(_mm(kk * kk, avg_ref[...]) * float(RWKV_N) + 1e-12)
        k2 = k * (1.0 + (a - 1.0) * ka_ref[...])
        beta = kk * a
        cw = _mm(tri_ref[...], lw, na=1, nb=2)
        e_in = jnp.exp(cw)
        e_out = jnp.exp(-cw)
        at = -kk * jnp.exp(cw - lw)
        rt = r * e_in
        bt = beta * e_out
        kt = k2 * e_out
        n = HEADS * c
        hs_a = _head_stack(at, hl).astype(BF16)
        hs_r = _head_stack(rt, hl)
        lhs_ar = jnp.concatenate([hs_a, hs_r.astype(BF16)], axis=0)
        rhs_bk = jnp.concatenate([bt] * HEADS + [kt] * HEADS, axis=0).astype(BF16)
        yield
        gram = mm(lhs_ar, rhs_bk, _NT)
        a_ab = gram[0:n, 0:n] * ms_ref[...]
        a_ak = (gram[0:n, n:2 * n] * ms_ref[...]).astype(BF16)
        a_rb = (gram[n:2 * n, 0:n] * mi_ref[...]).astype(BF16)
        a_rk = (gram[n:2 * n, n:2 * n] * mi_ref[...]).astype(BF16)
        tinv = eye_ref[...] + a_ab
        pw = a_ab.astype(BF16)
        yield
        pw = mm(pw, pw).astype(BF16)
        v_st = _head_stack(v, hl).astype(BF16)
        akrk_v = mm(jnp.concatenate([a_ak, a_rk], axis=0), v_st)
        av = akrk_v[0:n].astype(BF16)
        for _ in range(int(math.log2(c)) - 2):
            yield
            both = mm(jnp.concatenate([pw, tinv.astype(BF16)], axis=0), pw)
            tinv = tinv + both[n:2 * n]
            pw = both[0:n].astype(BF16)
        yield
        tinv = (tinv + mm(tinv, pw)).astype(BF16)
        yield
        w12 = mm(tinv, jnp.concatenate([av, hs_a], axis=1)).astype(BF16)
        yield
        aw = mm(a_rb, w12)
        cwl = cw[c - 1:c, :]
        dec = jnp.exp(cwl - cw)
        hs_bh = _head_stack(beta * dec, hl).astype(BF16)
        hs_kh = _head_stack(k2 * dec, hl).astype(BF16)
        bw = mm(hs_bh, w12, _TN)
        yield
        st = st_ref[bi].astype(BF16)
        wv = BRANCH_WIDTH
        y_st = mm(hs_r + aw[:, wv:2 * wv], st) + aw[:, 0:wv] + akrk_v[n:2 * n]
        y = y_st[0:c] + y_st[c:2 * c] + y_st[2 * c:3 * c] + y_st[3 * c:4 * c]
        m_t = eye_ref[...] * jnp.exp(cwl) + bw[:, wv:2 * wv]
        st_ref[bi] = mm(m_t, st) + bw[:, 0:wv] + mm(hs_kh, v_st, _TN)
        yield
        mean = _seg_mean(y, avg_ref[...])
        d = y - mean
        var = _seg_mean(d * d, avg_ref[...])
        yn = d * lax.rsqrt(var + RWKV_GN_EPS) * lnw_ref[...] + lnb_ref[...]
        bonus = _mm(r * k2 * rk_ref[...], avg_ref[...]) * float(RWKV_N) * v
        o_ref[bi, trows, :] = ((yn + bonus) * g).astype(o_ref.dtype)

    def chunk(ci, carry):
        _round_robin([one(ci, bi) for bi in range(bb)])
        return carry

    lax.fori_loop(0, ts // c, chunk, 0, unroll=2)


def _rwkv_call(xb, w, mu, w0, w2, a0, a2, g2, k_k, k_a, r_k, ln_w, ln_b, avg):
    b, s, d = xb.shape
    ts, c = TIME_TILE, RWKV_CHUNK
    n = HEADS * c
    tri = jnp.asarray(np.tril(np.ones((c, c), np.float32))).astype(BF16)
    ridx = np.arange(n)
    same = (ridx[:, None] // c) == (ridx[None, :] // c)
    ms = jnp.asarray((same & ((ridx[None, :] % c) < (ridx[:, None] % c))).astype(np.float32))
    mi = jnp.asarray((same & ((ridx[None, :] % c) <= (ridx[:, None] % c))).astype(np.float32))
    eye = jnp.asarray(np.eye(n, dtype=np.float32))
    row = lambda a: a.reshape(1, -1).astype(F32)
    args = (row(mu), row(w0), w2.astype(F32), row(a0), a2.astype(F32), g2.astype(F32),
            row(k_k), row(k_a), row(r_k), row(ln_w), row(ln_b), tri, ms, mi, eye, avg)
    body = functools.partial(_rwkv_body, c=c)
    return _mixer_pallas(body, "rwkv7", xb, w, args, [_const_spec(a.shape) for a in args], (n, BRANCH_WIDTH),
                         RWKV_SEQ_PER_STEP, extra_scratch=[pltpu.VMEM((RWKV_SEQ_PER_STEP, 8, w.shape[1]), F32)])


def _merge_body(x32_ref, xb_ref, b0_ref, b1_ref, b2_ref, b3_ref, wg_ref, wb_ref, wm_ref, lw_ref, lb_ref,
                o32_ref, o16_ref):
    tm = x32_ref.shape[0]
    for r0 in range(0, tm, DENSE_SUB_ROWS):
        rows = slice(r0, r0 + DENSE_SUB_ROWS)
        xb = xb_ref[rows, :]
        acc = None
        for n, br_ref in enumerate((b0_ref, b1_ref, b2_ref, b3_ref)):
            gate = jnp.dot(xb, wg_ref[:, n * D_MODEL:(n + 1) * D_MODEL], preferred_element_type=F32)
            br = jnp.dot(br_ref[rows, :], wb_ref[n], preferred_element_type=F32)
            t = _sigmoid(gate) * br
            acc = t if acc is None else acc + t
        y = jnp.dot(acc.astype(BF16), wm_ref[...], preferred_element_type=F32)
        out = _layer_norm_rows(ALPHA * x32_ref[rows, :] + y, lw_ref[...], lb_ref[...])
        o32_ref[rows, :] = out
        o16_ref[rows, :] = out.astype(BF16)


def _merge_call(x32, xb, branches, wg, wb, wm, ln_w, ln_b):
    t, d = x32.shape
    tm = ROW_TILE
    row_spec = lambda width: pl.BlockSpec((tm, width), lambda i: (i, 0))
    full = lambda a: pl.BlockSpec(a.shape, lambda i: (0,) * a.ndim)
    lw = ln_w.reshape(1, d)
    lb = ln_b.reshape(1, d)
    return pl.pallas_call(
        _merge_body,
        grid=(t // tm,),
        in_specs=[row_spec(d), row_spec(d)] + [row_spec(BRANCH_WIDTH)] * N_BRANCH
                 + [full(wg), full(wb), full(wm), full(lw), full(lb)],
        out_specs=[row_spec(d), row_spec(d)],
        out_shape=[jax.ShapeDtypeStruct((t, d), F32), jax.ShapeDtypeStruct((t, d), BF16)],
        compiler_params=pltpu.CompilerParams(dimension_semantics=("parallel",),
                                             vmem_limit_bytes=V7X_VMEM_LIMIT_BYTES),
        name="merge",
    )(x32, xb, *branches, wg, wb, wm, lw, lb)


FFN_CHUNK = 1408


def _ffn_body(x32_ref, xb_ref, p_ref, wup_ref, cw_ref, wdn_ref, wpg_ref, wpp_ref, lw_ref, lb_ref,
              o32_ref, o16_ref, carry_ref, *, tm):
    @pl.when(pl.program_id(1) == 0)
    def _():
        carry_ref[...] = jnp.zeros_like(carry_ref)

    sub = DENSE_SUB_ROWS
    rowi = lax.broadcasted_iota(jnp.int32, (sub, 1), 0)
    chunks = [(ci * FFN_CHUNK, (ci + 1) * FFN_CHUNK) for ci in range(D_FF // FFN_CHUNK)]
    prev_u = [carry_ref[:, lo:hi] for lo, hi in chunks]
    for r0 in range(0, tm, sub):
        rows = slice(r0, r0 + sub)
        xb = xb_ref[0, rows, :]
        ups = [(jnp.dot(xb, wup_ref[:, lo:hi], preferred_element_type=F32),
                jnp.dot(xb, wup_ref[:, D_FF + lo:D_FF + hi], preferred_element_type=F32)) for lo, hi in chunks]
        hidden = []
        for ci, ((lo, hi), (u, v)) in enumerate(zip(chunks, ups)):
            p1 = jnp.broadcast_to(prev_u[ci][7:8, :], u.shape)
            p2 = jnp.broadcast_to(prev_u[ci][6:7, :], u.shape)
            u1 = jnp.where(rowi == 0, p1, pltpu.roll(u, 1, 0))
            u2 = jnp.where(rowi == 0, p2, jnp.where(rowi == 1, p1, pltpu.roll(u, 2, 0)))
            prev_u[ci] = u[sub - 8:sub, :]
            uc = cw_ref[0:1, lo:hi] * u2 + cw_ref[1:2, lo:hi] * u1 + cw_ref[2:3, lo:hi] * u
            hidden.append((_gelu_tanh(uc) * v).astype(BF16))
        acc = jnp.dot(hidden[0], wdn_ref[chunks[0][0]:chunks[0][1], :], preferred_element_type=F32)
        ple = _sigmoid(jnp.dot(xb, wpg_ref[...], preferred_element_type=F32)) * jnp.dot(
            p_ref[0, rows, :].astype(BF16), wpp_ref[...], preferred_element_type=F32)
        for (lo, hi), h in zip(chunks[1:], hidden[1:]):
            acc = acc + jnp.dot(h, wdn_ref[lo:hi, :], preferred_element_type=F32)
        out = _layer_norm_rows(ALPHA * x32_ref[0, rows, :] + acc + ple, lw_ref[...], lb_ref[...])
        o32_ref[0, rows, :] = out
        o16_ref[0, rows, :] = out.astype(BF16)
    for (lo, hi), u_tail in zip(chunks, prev_u):
        carry_ref[:, lo:hi] = u_tail


def _ffn_call(x32, xb, p_all, layer, wup, convw, wdn, wpg, wpp, ln_w, ln_b):
    b, s, d = x32.shape
    tm = ROW_TILE
    row_spec = lambda width: pl.BlockSpec((1, tm, width), lambda i, t: (i, t, 0))
    p_spec = pl.BlockSpec((None, 1, tm, PLE_DIM), lambda i, t: (layer, i, t, 0))
    full = lambda a: pl.BlockSpec(a.shape, lambda i, t: (0,) * a.ndim, pipeline_mode=pl.Buffered(1))
    lw = ln_w.reshape(1, d)
    lb = ln_b.reshape(1, d)
    body = functools.partial(_ffn_body, tm=tm)
    return pl.pallas_call(
        body,
        grid=(b, s // tm),
        in_specs=[row_spec(d), row_spec(d), p_spec]
                 + [full(wup), full(convw), full(wdn), full(wpg), full(wpp), full(lw), full(lb)],
        out_specs=[row_spec(d), row_spec(d)],
        out_shape=[jax.ShapeDtypeStruct((b, s, d), F32), jax.ShapeDtypeStruct((b, s, d), BF16)],
        scratch_shapes=[pltpu.VMEM((8, D_FF), F32)],
        compiler_params=pltpu.CompilerParams(dimension_semantics=("parallel", "arbitrary"),
                                             vmem_limit_bytes=V7X_VMEM_LIMIT_BYTES),
        name="conv_ffn",
    )(x32, xb, p_all, wup, convw, wdn, wpg, wpp, lw, lb)


def _group_weights(w_in_l):
    offs = np.cumsum((0,) + GROUP_COLS)
    w_ret = w_in_l[:, offs[0]:offs[1]]
    w_rwkv = w_in_l[:, offs[1]:offs[2]]
    w_gla = w_in_l[:, offs[2]:offs[3]]
    w_hgrn = w_in_l[:, offs[3]:offs[4]]
    w_gate = w_in_l[:, offs[4]:offs[5]]
    perm = _ret_perm()
    nq = HEADS * RET_DK
    w_ret = jnp.concatenate([w_ret[:, perm], w_ret[:, nq + perm], w_ret[:, 2 * nq:]], axis=1)
    gq = HEADS * GLA_DK
    o_v, o_gl, o_g = 2 * gq, 2 * gq + BRANCH_WIDTH, 2 * gq + BRANCH_WIDTH + GLA_GATE_LORA
    pad = jnp.zeros((w_gla.shape[0], LANES - GLA_GATE_LORA), w_gla.dtype)
    w_gla = jnp.concatenate([w_gla[:, :o_gl], w_gla[:, o_g:], w_gla[:, o_gl:o_g], pad], axis=1)
    cast = lambda a: a.astype(BF16)
    return cast(w_ret), cast(w_rwkv), cast(w_gla), cast(w_hgrn), cast(w_gate)


def kernel(x, p, ln_in_w, ln_in_b, w_in, rwkv_mu, rwkv_w0, rwkv_w2, rwkv_a0, rwkv_a2, rwkv_g2, rwkv_k_k, rwkv_k_a, rwkv_r_k, rwkv_ln_w, rwkv_ln_b, gla_w2, gla_b, gla_norm_w, hgrn_lb_logits, hgrn_norm_w, w_branch, w_mix_out, ln_mix_w, ln_mix_b, w_ffn_up, ffn_conv, w_ffn_down, w_ple_gate, w_ple_proj, ln_ffn_w, ln_ffn_b):
    b, s, d = x.shape
    depth = w_in.shape[0]
    t = b * s
    v_head = np.arange(BRANCH_WIDTH) // HEAD_DV
    avg = jnp.asarray((v_head[:, None] == v_head[None, :]).astype(np.float32) / HEAD_DV).astype(BF16)
    rope = _rope_tables(s)

    x32, xb = _ln_call(x.reshape(t, d), ln_in_w, ln_in_b)
    for i in range(depth):
        w_ret, w_rwkv, w_gla, w_hgrn, w_gate = _group_weights(w_in[i])
        xb3 = xb.reshape(b, s, d)
        o_ret = _ret_call(xb3, w_ret, rope, avg)
        o_rwkv = _rwkv_call(xb3, w_rwkv, rwkv_mu[i], rwkv_w0[i], rwkv_w2[i], rwkv_a0[i], rwkv_a2[i],
                            rwkv_g2[i], rwkv_k_k[i], rwkv_k_a[i], rwkv_r_k[i], rwkv_ln_w[i], rwkv_ln_b[i], avg)
        w2p = jnp.concatenate([gla_w2[i], jnp.zeros((LANES - GLA_GATE_LORA, gla_w2.shape[2]), F32)], axis=0)
        o_gla = _gla_call(xb3, w_gla, w2p, gla_b[i].reshape(1, -1), jnp.tile(gla_norm_w[i], HEADS).reshape(1, -1), avg)
        o_hgrn = _hgrn_call(xb3, w_hgrn, hgrn_lb_logits.astype(F32), jnp.tile(hgrn_norm_w[i], HEADS).reshape(1, -1),
                            avg, i)
        branches = [o.reshape(t, BRANCH_WIDTH) for o in (o_ret, o_rwkv, o_gla, o_hgrn)]
        x32, xb = _merge_call(x32, xb, branches, w_gate, w_branch[i].astype(BF16), w_mix_out[i].astype(BF16),
                              ln_mix_w[i], ln_mix_b[i])
        x32, xb = _ffn_call(x32.reshape(b, s, d), xb.reshape(b, s, d), p, i, w_ffn_up[i].astype(BF16), ffn_conv[i],
                            w_ffn_down[i].astype(BF16), w_ple_gate[i].astype(BF16), w_ple_proj[i].astype(BF16),
                            ln_ffn_w[i], ln_ffn_b[i])
        x32 = x32.reshape(t, d)
        xb = xb.reshape(t, d)
    return x32.reshape(b, s, d)
```
